```python
import math
import jax
import jax.numpy as jnp
from jax import lax
import numpy as np

D_MODEL = 1024
BATCH = 2
SEQ = 16384
DEPTH = 4

GRID_W = 64
CTX_LEN = 256
N_MIXERS = 3
N_MOD = 6
EPS = 1e-6

CONV_WIDTH = 31

HY_ORDER = 2
HY_SHORT = 3
HY_EMB = 33
HY_FILTER_WIDTH = 64
HY_INNER = 2
HY_TARGET = 1e-2
HY_DECAY_MIN = math.log(HY_TARGET) / 1.5
HY_DECAY_MAX = math.log(HY_TARGET) / 0.3

HG_EXPAND = 128
HG_HEADS = D_MODEL // HG_EXPAND
HG_HEAD_I = D_MODEL // HG_HEADS
HG_CHUNK = 64

D_FF = 7 * D_MODEL // 2
N_EXPERTS = 8
TOP_K = 2
MOE_BLOCK = 256

N_A = (DEPTH + 2) // 3
N_B = (DEPTH + 1) // 3
N_C = DEPTH // 3
N_DENSE = (DEPTH + 1) // 2
N_MOE = DEPTH // 2

kernel_name = "hybrid_conformer_hyena_hgrn2_moe_trunk"


def rms_norm(x, g):
    xf = x.astype(jnp.float32)
    y = xf * lax.rsqrt(jnp.mean(xf * xf, axis=-1, keepdims=True) + EPS)
    return (y * g.astype(jnp.float32)).astype(x.dtype)


def layer_norm(x, g, b):
    xf = x.astype(jnp.float32)
    mu = jnp.mean(xf, axis=-1, keepdims=True)
    xc = xf - mu
    var = jnp.mean(xc * xc, axis=-1, keepdims=True)
    return (xc * lax.rsqrt(var + EPS) * g.astype(jnp.float32) + b.astype(jnp.float32)).astype(x.dtype)


def pos_embed_2d(rows, dim):
    quarter = dim // 4
    omega = 1.0 / (10000.0 ** (jnp.arange(quarter, dtype=jnp.float32) / quarter))
    t = jnp.arange(rows * GRID_W)
    r = (t // GRID_W).astype(jnp.float32)[:, None] * omega
    col = (t % GRID_W).astype(jnp.float32)[:, None] * omega
    return jnp.concatenate([jnp.sin(r), jnp.cos(r), jnp.sin(col), jnp.cos(col)], axis=-1)


def depthwise_conv(x, w, b):
    width = w.shape[0]
    pad = (width - 1) // 2
    y = lax.conv_general_dilated(
        x, w[:, None, :].astype(x.dtype), window_strides=(1,),
        padding=[(pad, width - 1 - pad)],
        dimension_numbers=("NWC", "WIO", "NWC"),
        feature_group_count=x.shape[-1])
    return y + b


def conformer_conv(h, w_pw1, b_pw1, w_dw, b_dw, ln_g, ln_b, w_pw2, b_pw2):
    a, g = jnp.split(h @ w_pw1 + b_pw1, 2, axis=-1)
    u = a * jax.nn.sigmoid(g)
    u = depthwise_conv(u, w_dw, b_dw)
    u = jax.nn.silu(layer_norm(u, ln_g, ln_b))
    return u @ w_pw2 + b_pw2


def hyena_filters(L, w_f1, b_f1, freq, w_f2, b_f2, w_f3):
    f32 = jnp.float32
    t = jnp.linspace(0.0, 1.0, L, dtype=f32)[:, None]
    bands = (HY_EMB - 1) // 2
    w = 2.0 * math.pi * jnp.arange(L, dtype=f32)[:, None] / L
    fb = jnp.linspace(1e-4, bands - 1, bands, dtype=f32)[None, :]
    z = jnp.concatenate([t, jnp.cos(fb * w), -jnp.sin(fb * w)], axis=-1)
    fr = freq.astype(f32)
    hid = jnp.sin(fr * (z @ w_f1.astype(f32) + b_f1.astype(f32)))
    for j in range(HY_INNER):
        hid = jnp.sin(fr * (hid @ w_f2[j].astype(f32) + b_f2[j].astype(f32)))
    filt = hid @ w_f3.astype(f32)
    d = filt.shape[-1] // (2 * HY_ORDER)
    deltas = jnp.abs(jnp.linspace(HY_DECAY_MIN, HY_DECAY_MAX, d, dtype=f32))
    decay = jnp.exp(-t * deltas)
    filt = filt.reshape(L, HY_ORDER, 2, d) * decay[:, None, None, :]
    fwd, bwd = filt[:, :, 0], filt[:, :, 1]
    two_sided = jnp.concatenate([fwd, jnp.zeros((1, HY_ORDER, d), f32), bwd[:0:-1]], axis=0)
    return two_sided / jnp.sum(jnp.abs(two_sided), axis=0, keepdims=True)


def hyena(h, w_in, b_in, w_short, b_short, w_f1, b_f1, freq, w_f2, b_f2, w_f3, bias, w_out, b_out):
    L = h.shape[1]
    u = depthwise_conv(h @ w_in + b_in, w_short, b_short)
    v, x1, x2 = jnp.split(u, 3, axis=-1)
    filt_f = jnp.fft.rfft(hyena_filters(L, w_f1, b_f1, freq, w_f2, b_f2, w_f3), axis=0)
    z = v.astype(jnp.float32)
    for n, gate in enumerate((x1, x2)):
        z_f = jnp.fft.rfft(z, n=2 * L, axis=1)
        conv = jnp.fft.irfft(z_f * filt_f[None, :, n], n=2 * L, axis=1)[:, :L]
        z = gate.astype(jnp.float32) * (conv + bias[n].astype(jnp.float32) * z)
    return z.astype(h.dtype) @ w_out + b_out


def gla_chunk_scan(q, k, v, g, s0):
    Bn, H, L, dk = q.shape
    dv = v.shape[-1]
    n = L // HG_CHUNK
    q, k, g = (a.reshape(Bn, H, n, HG_CHUNK, dk) for a in (q, k, g))
    v = v.reshape(Bn, H, n, HG_CHUNK, dv)
    b = jnp.cumsum(g, axis=3)
    b_last = b[:, :, :, -1:]
    q_dec = q * jnp.exp(b)
    k_inv = k * jnp.exp(-b)
    k_end = k * jnp.exp(b_last - b)
    mask = jnp.tril(jnp.ones((HG_CHUNK, HG_CHUNK), dtype=bool))
    att = jnp.where(mask, jnp.einsum("bhntk,bhnsk->bhnts", q_dec, k_inv), 0.0)
    o_intra = jnp.einsum("bhnts,bhnsv->bhntv", att, v)
    ds = jnp.einsum("bhnsk,bhnsv->bhnkv", k_end, v)
    chunk_decay = jnp.exp(b_last[:, :, :, 0])

    def step(s, inp):
        dec, d = inp
        return dec[..., None] * s + d, s

    s_final, s_prev = lax.scan(step, s0, (jnp.moveaxis(chunk_decay, 2, 0), jnp.moveaxis(ds, 2, 0)))
    s_prev = jnp.moveaxis(s_prev, 0, 2)
    o_inter = jnp.einsum("bhntk,bhnkv->bhntv", q_dec, s_prev)
    return (o_intra + o_inter).reshape(Bn, H, L, dv), s_final


def hgrn2_mixer(h, hc, lb, w_in, gn_g, w_out):
    f32 = jnp.float32
    lb = lb.astype(f32)

    def flip(a):
        return jnp.flip(a, axis=2)

    def mix(u, states):
        Bn, L, D = u.shape
        q, i, og, zf, zb = jnp.split(u @ w_in, 5, axis=-1)

        def heads(a):
            return a.astype(f32).reshape(Bn, L, HG_HEADS, -1).transpose(0, 2, 1, 3)

        qh, vh = heads(jax.nn.silu(q)), heads(i)
        f_fw = lb + (1.0 - lb) * jax.nn.sigmoid(zf.astype(f32))
        f_bw = lb + (1.0 - lb) * jax.nn.sigmoid(zb.astype(f32))
        o_f, s_f = gla_chunk_scan(qh, heads(1.0 - f_fw), vh, heads(jnp.log(f_fw)), states[0])
        o_b, s_b = gla_chunk_scan(flip(qh), flip(heads(1.0 - f_bw)), flip(vh), flip(heads(jnp.log(f_bw))), states[1])
        o = (o_f + flip(o_b)).transpose(0, 2, 1, 3)
        o = o * lax.rsqrt(jnp.mean(o * o, axis=-1, keepdims=True) + EPS)
        o = o * gn_g.astype(f32).reshape(HG_HEADS, HG_HEAD_I)
        y = (o.reshape(Bn, L, D).astype(u.dtype) * jax.nn.silu(og)) @ w_out
        return y, (s_f, s_b)

    zero = jnp.zeros((hc.shape[0], HG_HEADS, HG_EXPAND, HG_HEAD_I), f32)
    yc, ctx_states = mix(hc, (zero, zero))
    y, _ = mix(h, ctx_states)
    return y, yc


def swiglu(t, w_gate, w_up, w_down):
    return (jax.nn.silu(t @ w_gate) * (t @ w_up)) @ w_down


def moe_swiglu(tok, w_router, w_gate, w_up, w_down):
    T, D = tok.shape
    logits = (tok @ w_router).astype(jnp.float32)
    top_logit, top_e = lax.top_k(logits, TOP_K)
    gate = jax.nn.softmax(top_logit, axis=-1)
    flat_e = top_e.reshape(-1)
    order = jnp.argsort(flat_e)
    sorted_e = flat_e[order]
    sorted_tok = (order // TOP_K).astype(jnp.int32)
    sorted_gate = gate.reshape(-1)[order]
    counts = jnp.bincount(flat_e, length=N_EXPERTS)
    padded = (counts + MOE_BLOCK - 1) // MOE_BLOCK * MOE_BLOCK
    start = jnp.cumsum(counts) - counts
    pad_end = jnp.cumsum(padded)
    pad_start = pad_end - padded
    dest = pad_start[sorted_e] + jnp.arange(T * TOP_K) - start[sorted_e]
    n_blocks = -(-(T * TOP_K) // MOE_BLOCK) + N_EXPERTS
    n_slots = n_blocks * MOE_BLOCK
    slot_tok = jnp.zeros((n_slots,), jnp.int32).at[dest].set(sorted_tok)
    slot_gate = jnp.zeros((n_slots,), jnp.float32).at[dest].set(sorted_gate)
    block_e = jnp.minimum(
        jnp.searchsorted(pad_end, jnp.arange(n_blocks) * MOE_BLOCK, side="right"), N_EXPERTS - 1)
    xb = tok[slot_tok].reshape(n_blocks, MOE_BLOCK, D)

    def expert_block(args):
        xe, e = args
        return swiglu(xe, w_gate[e], w_up[e], w_down[e])

    yb = lax.map(expert_block, (xb, block_e)).reshape(n_slots, D)
    yb = yb * slot_gate[:, None].astype(yb.dtype)
    return jnp.zeros_like(tok).at[slot_tok].add(yb)


def setup_inputs(seed: int = 0) -> dict:
    key = jax.random.key(seed)
    ks = iter(jax.random.split(key, 48))
    f32 = jnp.float32
    D = D_MODEL

    def w(shape, fan_in, gain=1.0):
        return jax.random.normal(next(ks), shape, f32) * (gain * fan_in ** -0.5)

    def small(shape, s=0.02):
        return jax.random.normal(next(ks), shape, f32) * s

    def ones_noise(shape):
        return 1.0 + small(shape)

    return {
        "x": jax.random.normal(next(ks), (BATCH, SEQ, D), f32),
        "c": jax.random.normal(next(ks), (BATCH, D), f32),
        "ctx": jax.random.normal(next(ks), (BATCH, CTX_LEN, D), f32),
        "c_ctx": jax.random.normal(next(ks), (D,), f32),
        "ada_w": w((DEPTH, D, N_MOD * D), D, 0.5),
        "ada_b": small((DEPTH, N_MOD * D)),
        "norm1_g": ones_noise((DEPTH, D)),
        "norm2_g": ones_noise((DEPTH, D)),
        "normf_g": ones_noise((D,)),
        "cf_w_pw1": w((N_A, D, 2 * D), D),
        "cf_b_pw1": small((N_A, 2 * D)),
        "cf_w_dw": w((N_A, CONV_WIDTH, D), CONV_WIDTH),
        "cf_b_dw": small((N_A, D)),
        "cf_ln_g": ones_noise((N_A, D)),
        "cf_ln_b": small((N_A, D)),
        "cf_w_pw2": w((N_A, D, D), D),
        "cf_b_pw2": small((N_A, D)),
        "hy_w_in": w((N_B, D, 3 * D), D),
        "hy_b_in": small((N_B, 3 * D)),
        "hy_w_short": w((N_B, HY_SHORT, 3 * D), HY_SHORT),
        "hy_b_short": small((N_B, 3 * D)),
        "hy_w_f1": w((N_B, HY_EMB, HY_FILTER_WIDTH), HY_EMB),
        "hy_b_f1": small((N_B, HY_FILTER_WIDTH)),
        "hy_freq": ones_noise((N_B, HY_FILTER_WIDTH)),
        "hy_w_f2": w((N_B, HY_INNER, HY_FILTER_WIDTH, HY_FILTER_WIDTH), HY_FILTER_WIDTH),
        "hy_b_f2": small((N_B, HY_INNER, HY_FILTER_WIDTH)),
        "hy_w_f3": w((N_B, HY_FILTER_WIDTH, HY_ORDER * 2 * D), HY_FILTER_WIDTH),
        "hy_bias": small((N_B, HY_ORDER, D), 0.5),
        "hy_w_out": w((N_B, D, D), D),
        "hy_b_out": small((N_B, D)),
        "hg_lb_logits": small((DEPTH, D), 0.1),
        "hg_w_in": w((N_C, D, 5 * D), D),
        "hg_gn_g": ones_noise((N_C, D)),
        "hg_w_out": w((N_C, D, D), D),
        "ffn_w_gate": w((N_DENSE, D, D_FF), D),
        "ffn_w_up": w((N_DENSE, D, D_FF), D),
        "ffn_w_down": w((N_DENSE, D_FF, D), D_FF),
        "moe_w_router": w((N_MOE, D, N_EXPERTS), D),
        "moe_w_gate": w((N_MOE, N_EXPERTS, D, D_FF), D),
        "moe_w_up": w((N_MOE, N_EXPERTS, D, D_FF), D),
        "moe_w_down": w((N_MOE, N_EXPERTS, D_FF, D), D_FF),
    }


def reference(x, c, ctx, c_ctx, ada_w, ada_b, norm1_g, norm2_g, normf_g,
              cf_w_pw1, cf_b_pw1, cf_w_dw, cf_b_dw, cf_ln_g, cf_ln_b, cf_w_pw2, cf_b_pw2,
              hy_w_in, hy_b_in, hy_w_short, hy_b_short, hy_w_f1, hy_b_f1, hy_freq, hy_w_f2, hy_b_f2,
              hy_w_f3, hy_bias, hy_w_out, hy_b_out,
              hg_lb_logits, hg_w_in, hg_gn_g, hg_w_out,
              ffn_w_gate, ffn_w_up, ffn_w_down,
              moe_w_router, moe_w_gate, moe_w_up, moe_w_down):
    Bn, S, D = x.shape
    rows = S // GRID_W
    x = x + pos_embed_2d(rows, D).astype(x.dtype)[None]
    xc = ctx
    p = jax.nn.softmax(hg_lb_logits.astype(jnp.float32), axis=0)
    lb_all = jnp.cumsum(p, axis=0) - p[0]
    silu_c = jax.nn.silu(c)
    silu_cc = jax.nn.silu(c_ctx)

    for i in range(DEPTH):
        last = i == DEPTH - 1
        kind, j = i % N_MIXERS, i // N_MIXERS
        sh1, sc1, g1, sh2, sc2, g2 = jnp.split((silu_c @ ada_w[i] + ada_b[i])[:, None, :], N_MOD, axis=-1)
        csh1, csc1, cg1, csh2, csc2, cg2 = jnp.split(silu_cc @ ada_w[i] + ada_b[i], N_MOD, axis=-1)

        h = rms_norm(x, norm1_g[i]) * (1.0 + sc1) + sh1
        if (not last) or kind == 2:
            hc = rms_norm(xc, norm1_g[i]) * (1.0 + csc1) + csh1
        if kind == 0:
            prm = (cf_w_pw1[j], cf_b_pw1[j], cf_w_dw[j], cf_b_dw[j], cf_ln_g[j], cf_ln_b[j], cf_w_pw2[j], cf_b_pw2[j])
            y = conformer_conv(h, *prm)
            yc = None if last else conformer_conv(hc, *prm)
        elif kind == 1:
            prm = (hy_w_in[j], hy_b_in[j], hy_w_short[j], hy_b_short[j], hy_w_f1[j], hy_b_f1[j], hy_freq[j],
                   hy_w_f2[j], hy_b_f2[j], hy_w_f3[j], hy_bias[j], hy_w_out[j], hy_b_out[j])
            y = hyena(h, *prm)
            yc = None if last else hyena(hc, *prm)
        else:
            y, yc = hgrn2_mixer(h, hc, lb_all[i], hg_w_in[j], hg_gn_g[j], hg_w_out[j])
        x = x + g1 * y
        if not last:
            xc = xc + cg1 * yc

        tok = (rms_norm(x, norm2_g[i]) * (1.0 + sc2) + sh2).reshape(-1, D)
        if not last:
            tok_c = (rms_norm(xc, norm2_g[i]) * (1.0 + csc2) + csh2).reshape(-1, D)
            tok = jnp.concatenate([tok_c, tok], axis=0)
        if i % 2 == 0:
            out = swiglu(tok, ffn_w_gate[i // 2], ffn_w_up[i // 2], ffn_w_down[i // 2])
        else:
            out = moe_swiglu(tok, moe_w_router[i // 2], moe_w_gate[i // 2], moe_w_up[i // 2], moe_w_down[i // 2])
        n_ctx = tok.shape[0] - Bn * S
        x = x + g2 * out[n_ctx:].reshape(Bn, S, D)
        if not last:
            xc = xc + cg2 * out[:n_ctx].reshape(xc.shape)

    return rms_norm(x, normf_g)
```

```python
import functools
import math

import numpy as np
import jax
import jax.numpy as jnp
from jax import lax
from jax.experimental import pallas as pl
from jax.experimental.pallas import tpu as pltpu

F32 = jnp.float32
BF16 = jnp.bfloat16
EPS = 1e-6

GRID_W = 64
N_MOD = 6
CONV_WIDTH = 31
CONV_HALO = 16
HY_EMB = 33
HY_EMB_PAD = 40
HY_TARGET = 1e-2
HY_DECAY_MIN = math.log(HY_TARGET) / 1.5
HY_DECAY_MAX = math.log(HY_TARGET) / 0.3
HG_HEAD = 128
HG_CHUNK = 64
N_EXPERTS = 8
LANES = 128
VMEM_LIMIT = 56 * 1024 * 1024


def _cp(sem, vmem=VMEM_LIMIT):
    return pltpu.CompilerParams(dimension_semantics=sem, vmem_limit_bytes=vmem)


def _dot(a, b):
    return jnp.dot(a, b, preferred_element_type=F32)


def _dot_nt(a, b):
    return lax.dot_general(a, b, (((1,), (1,)), ((), ())), preferred_element_type=F32)


def _dot_tn(a, b):
    return lax.dot_general(a, b, (((0,), (0,)), ((), ())), preferred_element_type=F32)


def _split3(a):
    h = a.astype(BF16)
    r = a - h.astype(F32)
    m = r.astype(BF16)
    l = (r - m.astype(F32)).astype(BF16)
    return h, m, l


def _dot_acc(a, b, dot=_dot):
    ah, am, _ = _split3(a)
    bh, bm, _ = _split3(b)
    return dot(ah, bh) + (dot(ah, bm) + dot(am, bh))


def _silu(x):
    return x * jax.nn.sigmoid(x)


def _ada_kernel(c_ref, w_ref, b_ref, o_ref):
    s = _silu(c_ref[...])
    o_ref[0] = _dot_acc(s, w_ref[0]) + b_ref[0]


def _ada_mod(cvecs, ada_w, ada_b):
    depth, d, n = ada_w.shape
    tn = 1024
    return pl.pallas_call(
        _ada_kernel,
        grid=(depth, n // tn),
        in_specs=[pl.BlockSpec((8, d), lambda i, j: (0, 0)),
                  pl.BlockSpec((1, d, tn), lambda i, j: (i, 0, j)),
                  pl.BlockSpec((1, 1, tn), lambda i, j: (i, 0, j))],
        out_specs=pl.BlockSpec((1, 8, tn), lambda i, j: (i, 0, j)),
        out_shape=jax.ShapeDtypeStruct((depth, 8, n), F32),
        compiler_params=_cp(("parallel", "parallel")),
        name="ada_mod",
    )(cvecs, ada_w, ada_b.reshape(depth, 1, n))


def _embed_kernel(x_ref, om_ref, o_ref, pos_ref, *, tm):
    i = pl.program_id(0)

    @pl.when(pl.program_id(1) == 0)
    def _():
        q = om_ref.shape[1]
        t = (lax.broadcasted_iota(jnp.int32, (tm, 1), 0) + i * tm).astype(F32)
        r = jnp.floor(t / GRID_W)
        col = t - r * GRID_W
        a = r * om_ref[...]
        b = col * om_ref[...]
        pos_ref[:, 0 * q:1 * q] = jnp.sin(a)
        pos_ref[:, 1 * q:2 * q] = jnp.cos(a)
        pos_ref[:, 2 * q:3 * q] = jnp.sin(b)
        pos_ref[:, 3 * q:4 * q] = jnp.cos(b)

    o_ref[0] = x_ref[0] + pos_ref[...]


def _embed(x):
    bn, s, d = x.shape
    tm = min(512, s)
    q = d // 4
    omega = (1.0 / (10000.0 ** (jnp.arange(q, dtype=F32) / q))).reshape(1, q)
    return pl.pallas_call(
        functools.partial(_embed_kernel, tm=tm),
        grid=(s // tm, bn),
        in_specs=[pl.BlockSpec((1, tm, d), lambda i, b: (b, i, 0)),
                  pl.BlockSpec((1, q), lambda i, b: (0, 0))],
        out_specs=pl.BlockSpec((1, tm, d), lambda i, b: (b, i, 0)),
        out_shape=jax.ShapeDtypeStruct(x.shape, F32),
        scratch_shapes=[pltpu.VMEM((tm, d), F32)],
        compiler_params=_cp(("parallel", "arbitrary")),
        name="pos_embed",
    )(x, omega)


def _norm_mod(x, g, sc1p, sh):
    ms = jnp.mean(x * x, axis=-1, keepdims=True)
    return (x * lax.rsqrt(ms + EPS) * g) * sc1p + sh


def _nmm_kernel(*refs, nw, epi, transposed):
    x_ref, g_ref, sc_ref, sh_ref = refs[:4]
    w_refs = refs[4:4 + nw]
    b_refs = refs[4 + nw:4 + 2 * nw]
    o_ref = refs[4 + 2 * nw]
    hs_ref = refs[5 + 2 * nw]

    @pl.when(pl.program_id(2) == 0)
    def _():
        hs_ref[...] = _norm_mod(x_ref[0], g_ref[...], sc_ref[0], sh_ref[0]).astype(BF16)

    h = hs_ref[...]
    if transposed:
        r = _dot_nt(w_refs[0][...], h) + b_refs[0][...]
    else:
        outs = [_dot(h, w[...]) + b[...] for w, b in zip(w_refs, b_refs)]
        if epi == "plain":
            r = outs[0]
        elif epi == "glu":
            r = outs[0] * jax.nn.sigmoid(outs[1])
        else:
            r = _silu(outs[0]) * outs[1]
    o_ref[0] = r.astype(o_ref.dtype)


def _nmm(x, gain, sc1p, sh, ws, bs, *, epi="plain", out_dtype=F32, transposed=False, tn=512):
    bn, l, d = x.shape
    n = ws[0].shape[0] if transposed else ws[0].shape[1]
    tm = min(512, l)
    tn = min(tn, n)
    nw = len(ws)
    if transposed:
        w_specs = [pl.BlockSpec((tn, d), lambda b, i, j: (j, 0))]
        b_specs = [pl.BlockSpec((tn, 1), lambda b, i, j: (j, 0))]
        bs = [bs[0].reshape(n, 1)]
        out_spec = pl.BlockSpec((1, tn, tm), lambda b, i, j: (b, j, i))
        out_shape = jax.ShapeDtypeStruct((bn, n, l), out_dtype)
    else:
        w_specs = [pl.BlockSpec((d, tn), lambda b, i, j: (0, j)) for _ in ws]
        b_specs = [pl.BlockSpec((1, tn), lambda b, i, j: (0, j)) for _ in ws]
        bs = [b.reshape(1, n) for b in bs]
        out_spec = pl.BlockSpec((1, tm, tn), lambda b, i, j: (b, i, j))
        out_shape = jax.ShapeDtypeStruct((bn, l, n), out_dtype)
    vec = pl.BlockSpec((1, 1, d), lambda b, i, j: (b, 0, 0))
    return pl.pallas_call(
        functools.partial(_nmm_kernel, nw=nw, epi=epi, transposed=transposed),
        grid=(bn, l // tm, n // tn),
        in_specs=[pl.BlockSpec((1, tm, d), lambda b, i, j: (b, i, 0)),
                  pl.BlockSpec((1, d), lambda b, i, j: (0, 0)), vec, vec] + w_specs + b_specs,
        out_specs=out_spec,
        out_shape=out_shape,
        scratch_shapes=[pltpu.VMEM((tm, d), BF16)],
        compiler_params=_cp(("parallel", "parallel", "arbitrary")),
        name="norm_mod_matmul_" + ("t" if transposed else epi),
    )(x, gain.reshape(1, d), sc1p, sh, *ws, *bs)


def _mmres_kernel(a_ref, w_ref, b_ref, x_ref, g_ref, o_ref, *, trans_a):
    a = a_ref[0]
    y = (_dot_tn(a, w_ref[...]) if trans_a else _dot(a, w_ref[...])) + b_ref[...]
    o_ref[0] = x_ref[0] + g_ref[0] * y


def _mmres(a, w, bias, x, gate, *, trans_a=False):
    bn, l, n = x.shape
    k = w.shape[0]
    tm = min(512, l)
    a_spec = (pl.BlockSpec((1, k, tm), lambda b, i: (b, 0, i)) if trans_a
              else pl.BlockSpec((1, tm, k), lambda b, i: (b, i, 0)))
    return pl.pallas_call(
        functools.partial(_mmres_kernel, trans_a=trans_a),
        grid=(bn, l // tm),
        in_specs=[a_spec,
                  pl.BlockSpec((k, n), lambda b, i: (0, 0)),
                  pl.BlockSpec((1, n), lambda b, i: (0, 0)),
                  pl.BlockSpec((1, tm, n), lambda b, i: (b, i, 0)),
                  pl.BlockSpec((1, 1, n), lambda b, i: (b, 0, 0))],
        out_specs=pl.BlockSpec((1, tm, n), lambda b, i: (b, i, 0)),
        out_shape=jax.ShapeDtypeStruct(x.shape, F32),
        input_output_aliases={3: 0},
        compiler_params=_cp(("parallel", "parallel")),
        name="matmul_residual" + ("_ta" if trans_a else ""),
    )(a, w, bias.reshape(1, n), x, gate)


def _cf2_kernel(prev_ref, cur_ref, next_ref, wdw_ref, bdw_ref, lng_ref, lnb_ref, w2_ref, b2_ref,
                x_ref, g_ref, o_ref, buf_ref, cv_ref, *, tm):
    i = pl.program_id(1)
    nt = pl.num_programs(1)
    h = CONV_HALO
    buf_ref[0:h] = jnp.where(i > 0, prev_ref[0], 0.0)
    buf_ref[h:h + tm] = cur_ref[0]
    buf_ref[h + tm:2 * h + tm] = jnp.where(i < nt - 1, next_ref[0], 0.0)
    off = h - (CONV_WIDTH - 1) // 2
    rc = 8
    for c in range(tm // rc):
        acc = buf_ref[off + c * rc:off + c * rc + rc] * wdw_ref[0:1]
        for k in range(1, CONV_WIDTH):
            acc = acc + buf_ref[off + c * rc + k:off + c * rc + k + rc] * wdw_ref[k:k + 1]
        cv_ref[c * rc:(c + 1) * rc] = acc + bdw_ref[...]
    u = cv_ref[...]
    mu = jnp.mean(u, axis=-1, keepdims=True)
    uc = u - mu
    var = jnp.mean(uc * uc, axis=-1, keepdims=True)
    v = _silu(uc * lax.rsqrt(var + EPS) * lng_ref[...] + lnb_ref[...])
    y = _dot(v.astype(BF16), w2_ref[...]) + b2_ref[...]
    o_ref[0] = x_ref[0] + g_ref[0] * y


def _conformer_tail(u, w_dw, b_dw, ln_g, ln_b, w2, b2, x, gate):
    bn, l, d = x.shape
    tm = min(256, l)
    h = CONV_HALO
    nh = l // h
    r = tm // h
    row = lambda a: a.reshape(1, d)
    full = lambda shp: pl.BlockSpec(shp, lambda b, i: (0,) * len(shp))
    return pl.pallas_call(
        functools.partial(_cf2_kernel, tm=tm),
        grid=(bn, l // tm),
        in_specs=[pl.BlockSpec((1, h, d), lambda b, i: (b, jnp.maximum(i * r - 1, 0), 0)),
                  pl.BlockSpec((1, tm, d), lambda b, i: (b, i, 0)),
                  pl.BlockSpec((1, h, d), lambda b, i: (b, jnp.minimum((i + 1) * r, nh - 1), 0)),
                  full((CONV_WIDTH, d)), full((1, d)), full((1, d)), full((1, d)),
                  full((d, d)), full((1, d)),
                  pl.BlockSpec((1, tm, d), lambda b, i: (b, i, 0)),
                  pl.BlockSpec((1, 1, d), lambda b, i: (b, 0, 0))],
        out_specs=pl.BlockSpec((1, tm, d), lambda b, i: (b, i, 0)),
        out_shape=jax.ShapeDtypeStruct(x.shape, F32),
        scratch_shapes=[pltpu.VMEM((tm + 2 * h, d), F32), pltpu.VMEM((tm, d), F32)],
        input_output_aliases={9: 0},
        compiler_params=_cp(("parallel", "parallel")),
        name="conformer_tail",
    )(u, u, u, w_dw, row(b_dw), row(ln_g), row(ln_b), w2, row(b2), x, gate)


def _fft_plan(l):
    n2 = 256 if l >= 4096 else 128
    r = max(16, -(-l // n2))
    return n2, r, 2 * r, r * n2


def _fft_consts(n1, n2):
    r = n1 // 2
    n = n1 * n2
    k1 = np.arange(n1, dtype=np.float64)[:, None]
    a1 = 2.0 * np.pi * k1 * np.arange(r, dtype=np.float64)[None, :] / n1
    fc1 = np.concatenate([np.cos(a1), -np.sin(a1)], axis=0)
    at = 2.0 * np.pi * k1 * np.arange(n2, dtype=np.float64)[None, :] / n
    tw = np.stack([np.cos(at), -np.sin(at)], axis=0)
    a2 = 2.0 * np.pi * np.outer(np.arange(n2, dtype=np.float64), np.arange(n2, dtype=np.float64)) / n2
    f2 = np.concatenate([np.cos(a2), -np.sin(a2)], axis=1)
    g2 = np.concatenate([np.cos(a2), np.sin(a2)], axis=1)
    ai = 2.0 * np.pi * np.arange(r, dtype=np.float64)[:, None] * np.arange(n1, dtype=np.float64)[None, :] / n1
    c1i = np.concatenate([np.cos(ai), -np.sin(ai)], axis=1) / n
    return (jnp.asarray(fc1, BF16), jnp.asarray(tw, F32), jnp.asarray(f2, BF16),
            jnp.asarray(g2, BF16), jnp.asarray(c1i, BF16))


def _fft_fwd_rows(z, fc1, tr, ti):
    n1 = tr.shape[0]
    a = _dot(fc1, z.astype(BF16))
    ar, ai = a[:n1], a[n1:]
    return ar * tr - ai * ti, ar * ti + ai * tr


def _cplx_rows(p, ch, n1, n2):
    p4 = p.reshape(ch, 2, n1, 2 * n2)
    return p4[:, 0, :, :n2] - p4[:, 1, :, n2:], p4[:, 0, :, n2:] + p4[:, 1, :, :n2]


def _hyfilt_kernel(w1_ref, b1_ref, fr_ref, w2_ref, b2_ref, fb_ref, w3_ref, dl_ref, o_ref, hid_ref, *, l, lp):
    lane = lax.broadcasted_iota(jnp.int32, (1, lp), 1)
    pos = lane.astype(F32)
    t = pos / (l - 1)

    @pl.when((pl.program_id(0) == 0) & (pl.program_id(1) == 0))
    def _():
        w = (2.0 * math.pi) * pos / l
        arg = fb_ref[...] * w
        row = lax.broadcasted_iota(jnp.int32, (HY_EMB_PAD, 1), 0)
        bands = (HY_EMB - 1) // 2
        z = jnp.where(row == 0, t,
                      jnp.where(row <= bands, jnp.cos(arg),
                                jnp.where(row < HY_EMB, -jnp.sin(arg), 0.0)))
        fr = fr_ref[...]
        hid = jnp.sin(fr * (_dot_acc(w1_ref[...], z) + b1_ref[...]))
        for j in range(w2_ref.shape[0]):
            hid = jnp.sin(fr * (_dot_acc(w2_ref[j], hid) + b2_ref[j]))
        hid_ref[...] = hid

    decay = jnp.exp(-t * dl_ref[...])
    hid = hid_ref[...]
    valid = lane < l
    f0 = jnp.where(valid, _dot_acc(w3_ref[0, 0], hid) * decay, 0.0)
    f1 = jnp.where(valid & (lane > 0), _dot_acc(w3_ref[0, 1], hid) * decay, 0.0)
    norm = jnp.sum(jnp.abs(f0), axis=-1, keepdims=True) + jnp.sum(jnp.abs(f1), axis=-1, keepdims=True)
    o_ref[0, 0] = f0 / norm
    o_ref[0, 1] = f1 / norm


def _hyena_filters_t(l, lp, w_f1, b_f1, freq, w_f2, b_f2, w_f3, order, d):
    fw = w_f1.shape[1]
    bands = (HY_EMB - 1) // 2
    fb = jnp.linspace(1e-4, bands - 1, bands, dtype=F32)
    fbcol = jnp.concatenate([jnp.zeros((1,), F32), fb, fb, jnp.zeros((HY_EMB_PAD - HY_EMB,), F32)]).reshape(-1, 1)
    w1t = jnp.pad(w_f1.T, ((0, 0), (0, HY_EMB_PAD - HY_EMB)))
    w2t = jnp.swapaxes(w_f2, 1, 2)
    w3t = w_f3.T.reshape(order, 2, d, fw)
    deltas = jnp.abs(jnp.linspace(HY_DECAY_MIN, HY_DECAY_MAX, d, dtype=F32)).reshape(d, 1)
    tr = 32
    c2 = lambda shp: pl.BlockSpec(shp, lambda o, c: (0,) * len(shp))
    return pl.pallas_call(
        functools.partial(_hyfilt_kernel, l=l, lp=lp),
        grid=(order, d // tr),
        in_specs=[c2((fw, HY_EMB_PAD)), c2((fw, 1)), c2((fw, 1)), c2(w2t.shape), c2((w_f2.shape[0], fw, 1)),
                  c2((HY_EMB_PAD, 1)),
                  pl.BlockSpec((1, 2, tr, fw), lambda o, c: (o, 0, c, 0)),
                  pl.BlockSpec((tr, 1), lambda o, c: (c, 0))],
        out_specs=pl.BlockSpec((1, 2, tr, lp), lambda o, c: (o, 0, c, 0)),
        out_shape=jax.ShapeDtypeStruct((order, 2, d, lp), F32),
        scratch_shapes=[pltpu.VMEM((fw, lp), F32)],
        compiler_params=_cp(("arbitrary", "arbitrary")),
        name="hyena_filter",
    )(w1t, b_f1.reshape(fw, 1), freq.reshape(fw, 1), w2t, b_f2.reshape(-1, fw, 1), fbcol, w3t, deltas)


def _hyspec_kernel(f_ref, fc1_ref, tw_ref, f2_ref, o_ref, ys_ref, *, dc, n1, n2):
    tr, ti = tw_ref[0], tw_ref[1]
    fc1 = fc1_ref[...]

    def stage1(c, _):
        for s in range(2):
            yr, yi = _fft_fwd_rows(f_ref[0, s, c], fc1, tr, ti)
            ys_ref[s, c, :n1] = yr.astype(BF16)
            ys_ref[s, c, n1:] = yi.astype(BF16)
        return 0

    lax.fori_loop(0, dc, stage1, 0)
    zs = []
    for s in range(2):
        p = _dot(ys_ref[s].reshape(dc * 2 * n1, n2), f2_ref[...])
        zs.append(_cplx_rows(p, dc, n1, n2))
    o_ref[0, 0] = zs[0][0] + zs[1][0]
    o_ref[0, 1] = zs[0][1] - zs[1][1]


def _hyena_spectrum(filt, consts, n1, n2):
    order, _, d, lp = filt.shape
    r = n1 // 2
    dc = 4
    fc1, tw, f2, _, _ = consts
    c2 = lambda a: pl.BlockSpec(a.shape, lambda o, c: (0,) * a.ndim)
    return pl.pallas_call(
        functools.partial(_hyspec_kernel, dc=dc, n1=n1, n2=n2),
        grid=(order, d // dc),
        in_specs=[pl.BlockSpec((1, 2, dc, r, n2), lambda o, c: (o, 0, c, 0, 0)), c2(fc1), c2(tw), c2(f2)],
        out_specs=pl.BlockSpec((1, 2, dc, n1, n2), lambda o, c: (o, 0, c, 0, 0)),
        out_shape=jax.ShapeDtypeStruct((order, 2, d, n1, n2), F32),
        scratch_shapes=[pltpu.VMEM((2, dc, 2 * n1, n2), BF16)],
        compiler_params=_cp(("parallel", "parallel")),
        name="hyena_spectrum",
    )(filt.reshape(order, 2, d, r, n2), fc1, tw, f2)


def _hyconv_kernel(wsh_ref, hb_ref, v_ref, x1_ref, x2_ref, h_ref, fc1_ref, tw_ref, f2_ref, g2_ref, c1i_ref,
                   o_ref, cs_ref, ys_ref, vs_ref, *, dc, ch, r, n1, n2, d, l, order):
    cb = pl.program_id(0)
    tr, ti = tw_ref[0], tw_ref[1]
    row = lax.broadcasted_iota(jnp.int32, (r, n2), 0)
    lane = lax.broadcasted_iota(jnp.int32, (r, n2), 1)
    valid = (row * n2 + lane) < l

    def short_conv(c, _):
        chan = cb * dc + c
        for s, ref in enumerate((v_ref, x1_ref, x2_ref)):
            u = ref[0, 0, c]
            idx = s * d + chan
            a = pltpu.roll(u, 1, axis=1)
            prev = jnp.where(lane == 0, jnp.where(row == 0, 0.0, pltpu.roll(a, 1, axis=0)), a)
            b = pltpu.roll(u, n2 - 1, axis=1)
            nxt = jnp.where(lane == n2 - 1, jnp.where(row == r - 1, 0.0, pltpu.roll(b, r - 1, axis=0)), b)
            y = wsh_ref[0, idx] * prev + wsh_ref[1, idx] * u + wsh_ref[2, idx] * nxt + wsh_ref[3, idx]
            cs_ref[s, c] = jnp.where(valid, y, 0.0)
        return 0

    lax.fori_loop(0, dc, short_conv, 0)

    for n in range(order):
        def stage1(c, _):
            yr, yi = _fft_fwd_rows(cs_ref[0, c], fc1_ref[...], tr, ti)
            ys_ref[c, :n1] = yr.astype(BF16)
            ys_ref[c, n1:] = yi.astype(BF16)
            return 0

        lax.fori_loop(0, dc, stage1, 0)

        for c0 in range(0, dc, ch):
            p = _dot(ys_ref[c0:c0 + ch].reshape(ch * 2 * n1, n2), f2_ref[...])
            zr, zi = _cplx_rows(p, ch, n1, n2)
            hr, hi = h_ref[n, 0, c0:c0 + ch], h_ref[n, 1, c0:c0 + ch]
            vs_ref[c0:c0 + ch, :n1] = (zr * hr - zi * hi).astype(BF16)
            vs_ref[c0:c0 + ch, n1:] = (zr * hi + zi * hr).astype(BF16)
            q = _dot(vs_ref[c0:c0 + ch].reshape(ch * 2 * n1, n2), g2_ref[...])
            ur, ui = _cplx_rows(q, ch, n1, n2)
            ys_ref[c0:c0 + ch, :n1] = (ur * tr + ui * ti).astype(BF16)
            ys_ref[c0:c0 + ch, n1:] = (ui * tr - ur * ti).astype(BF16)

        def stage4(c, _):
            chan = cb * dc + c
            conv = _dot(c1i_ref[...], ys_ref[c])
            z = cs_ref[n + 1, c] * (conv + hb_ref[n, chan] * cs_ref[0, c])
            if n == order - 1:
                o_ref[0, c] = z.astype(o_ref.dtype)
            else:
                cs_ref[0, c] = z
            return 0

        lax.fori_loop(0, dc, stage4, 0)


def _hyena_conv(ut, w_short, b_short, hy_bias, spec, consts, n1, n2, l):
    bn, d3, lp = ut.shape
    d = d3 // 3
    order = hy_bias.shape[0]
    r = n1 // 2
    dc = 8
    ch = min(4, dc)
    fc1, tw, f2, g2, c1i = consts
    wsh = jnp.concatenate([w_short, b_short.reshape(1, d3)], axis=0)
    u5 = ut.reshape(bn, 3, d, r, n2)
    cst = lambda a: pl.BlockSpec(a.shape, lambda c, b, *_: (0,) * a.ndim)
    stream = lambda s: pl.BlockSpec((1, 1, dc, r, n2), lambda c, b, *_: (b, s, c, 0, 0))
    grid_spec = pltpu.PrefetchScalarGridSpec(
        num_scalar_prefetch=2,
        grid=(d // dc, bn),
        in_specs=[stream(0), stream(1), stream(2),
                  pl.BlockSpec((order, 2, dc, n1, n2), lambda c, b, *_: (0, 0, c, 0, 0)),
                  cst(fc1), cst(tw), cst(f2), cst(g2), cst(c1i)],
        out_specs=pl.BlockSpec((1, dc, r, n2), lambda c, b, *_: (b, c, 0, 0)),
        scratch_shapes=[pltpu.VMEM((3, dc, r, n2), F32),
                        pltpu.VMEM((dc, 2 * n1, n2), BF16),
                        pltpu.VMEM((dc, 2 * n1, n2), BF16)],
    )
    out = pl.pallas_call(
        functools.partial(_hyconv_kernel, dc=dc, ch=ch, r=r, n1=n1, n2=n2, d=d, l=l, order=order),
        grid_spec=grid_spec,
        out_shape=jax.ShapeDtypeStruct((bn, d, r, n2), BF16),
        compiler_params=_cp(("parallel", "arbitrary")),
        name="hyena_conv",
    )(wsh, hy_bias, u5, u5, u5, spec, fc1, tw, f2, g2, c1i)
    return out.reshape(bn, d, lp)


def _cumsum_acc(tri, g):
    gh, gm, gl = _split3(g)
    return _dot(tri, gh) + (_dot(tri, gm) + _dot(tri, gl))


def _scan_kernel(qf_ref, vf_ref, zf_ref, qb_ref, vb_ref, zb_ref, lb_ref, s0_ref,
                 of_ref, ob_ref, sf_ref, st_ref, *, tt):
    i = pl.program_id(2)
    ck = HG_CHUNK

    @pl.when(i == 0)
    def _():
        st_ref[...] = s0_ref[0, :, 0]

    lb = lb_ref[...]
    rr = lax.broadcasted_iota(jnp.int32, (ck, ck), 0)
    cc = lax.broadcasted_iota(jnp.int32, (ck, ck), 1)
    ones = jnp.ones((ck, HG_HEAD), BF16)

    def chunk(q_ref, v_ref, z_ref, o_ref, di, c):
        sl = slice(c * ck, (c + 1) * ck)
        f = lb + (1.0 - lb) * jax.nn.sigmoid(z_ref[0, sl])
        g = jnp.log(f)
        k = 1.0 - f
        keep = (rr >= cc) if di == 0 else (rr <= cc)
        b = _cumsum_acc(keep.astype(BF16), g)
        btot = b[ck - 1:ck] if di == 0 else b[0:1]
        qd = (_silu(q_ref[0, sl]) * jnp.exp(b)).astype(BF16)
        ki = (k * jnp.exp(-b)).astype(BF16)
        ke = (k * jnp.exp(btot - b)).astype(BF16)
        v = v_ref[0, sl].astype(BF16)
        att = jnp.where(keep, _dot_nt(qd, ki), 0.0)
        s = st_ref[di]
        o_ref[0, sl] = _dot(att.astype(BF16), v) + _dot(qd, s.astype(BF16))
        gh, gm, gl = _split3(g)
        dec = jnp.exp(_dot_tn(gh, ones) + (_dot_tn(gm, ones) + _dot_tn(gl, ones)))
        st_ref[di] = dec * s + _dot_tn(ke, v)

    nck = tt // ck
    for c in range(nck):
        chunk(qf_ref, vf_ref, zf_ref, of_ref, 0, c)
        chunk(qb_ref, vb_ref, zb_ref, ob_ref, 1, nck - 1 - c)

    @pl.when(i == pl.num_programs(2) - 1)
    def _():
        sf_ref[0, :, 0] = st_ref[...]


def _hgrn_scan(u5, lb, s0):
    bn, l, d5 = u5.shape
    d = d5 // 5
    nh = d // HG_HEAD
    tt = min(256, l)
    nt = l // tt
    col = lambda s, rev: pl.BlockSpec(
        (1, tt, HG_HEAD), (lambda b, h, i: (b, nt - 1 - i, s * nh + h)) if rev else (lambda b, h, i: (b, i, s * nh + h)))
    st_spec = pl.BlockSpec((1, 2, 1, HG_HEAD, HG_HEAD), lambda b, h, i: (b, 0, h, 0, 0))
    return pl.pallas_call(
        functools.partial(_scan_kernel, tt=tt),
        grid=(bn, nh, nt),
        in_specs=[col(0, False), col(1, False), col(3, False), col(0, True), col(1, True), col(4, True),
                  pl.BlockSpec((1, HG_HEAD), lambda b, h, i: (0, h)), st_spec],
        out_specs=[col(0, False), col(0, True), st_spec],
        out_shape=[jax.ShapeDtypeStruct((bn, l, d), F32), jax.ShapeDtypeStruct((bn, l, d), F32),
                   jax.ShapeDtypeStruct(s0.shape, F32)],
        scratch_shapes=[pltpu.VMEM((2, HG_HEAD, HG_HEAD), F32)],
        compiler_params=_cp(("parallel", "parallel", "arbitrary")),
        name="hgrn2_scan",
    )(u5, u5, u5, u5, u5, u5, lb.reshape(1, d), s0)


def _hgout_kernel(of_ref, ob_ref, og_ref, gn_ref, w_ref, x_ref, g_ref, o_ref, a_ref, *, nh):
    for h in range(nh):
        sl = slice(h * HG_HEAD, (h + 1) * HG_HEAD)
        o = of_ref[0, :, sl] + ob_ref[0, :, sl]
        o = o * lax.rsqrt(jnp.mean(o * o, axis=-1, keepdims=True) + EPS) * gn_ref[:, sl]
        a_ref[:, sl] = (o * _silu(og_ref[0, :, sl])).astype(BF16)
    o_ref[0] = x_ref[0] + g_ref[0] * _dot(a_ref[...], w_ref[...])


def _hgrn_out(o_f, o_b, u5, gn_g, w_out, x, gate):
    bn, l, d = x.shape
    nh = d // HG_HEAD
    tm = min(512, l)
    blk = lambda c: pl.BlockSpec((1, tm, d), lambda b, i: (b, i, c))
    return pl.pallas_call(
        functools.partial(_hgout_kernel, nh=nh),
        grid=(bn, l // tm),
        in_specs=[blk(0), blk(0), blk(2),
                  pl.BlockSpec((1, d), lambda b, i: (0, 0)),
                  pl.BlockSpec((d, d), lambda b, i: (0, 0)),
                  blk(0), pl.BlockSpec((1, 1, d), lambda b, i: (b, 0, 0))],
        out_specs=blk(0),
        out_shape=jax.ShapeDtypeStruct(x.shape, F32),
        scratch_shapes=[pltpu.VMEM((tm, d), BF16)],
        input_output_aliases={5: 0},
        compiler_params=_cp(("parallel", "parallel")),
        name="hgrn2_out",
    )(o_f, o_b, u5, gn_g.reshape(1, d), w_out, x, gate)


def _route_kernel(x_ref, g_ref, sc_ref, sh_ref, wr_ref, tok_ref, info_ref, cnt_ref, run_ref, *, tm):
    first = (pl.program_id(0) == 0) & (pl.program_id(1) == 0)

    @pl.when(first)
    def _():
        run_ref[...] = jnp.zeros_like(run_ref)

    h = _norm_mod(x_ref[0], g_ref[...], sc_ref[0], sh_ref[0])
    tok_ref[0] = h
    logits = _dot_acc(h, wr_ref[...])
    lane = lax.broadcasted_iota(jnp.int32, (tm, LANES), 1)
    lanef = lane.astype(F32)
    neg = jnp.float32(-jnp.inf)
    lg = jnp.where(lane < N_EXPERTS, logits, neg)
    m1 = jnp.max(lg, axis=-1, keepdims=True)
    i1 = jnp.min(jnp.where(lg == m1, lanef, float(LANES)), axis=-1, keepdims=True)
    lg2 = jnp.where(lanef == i1, neg, lg)
    m2 = jnp.max(lg2, axis=-1, keepdims=True)
    i2 = jnp.min(jnp.where(lg2 == m2, lanef, float(LANES)), axis=-1, keepdims=True)
    e2 = jnp.exp(m2 - m1)
    gate1 = 1.0 / (1.0 + e2)
    gate2 = e2 / (1.0 + e2)
    oh1 = (lanef == i1).astype(F32)
    oh2 = (lanef == i2).astype(F32)
    both = oh1 + oh2
    rr = lax.broadcasted_iota(jnp.int32, (tm, tm), 0)
    cc = lax.broadcasted_iota(jnp.int32, (tm, tm), 1)
    before = _dot((rr > cc).astype(BF16), both.astype(BF16)) + run_ref[...]
    r1 = jnp.sum(before * oh1, axis=-1, keepdims=True)
    r2 = jnp.sum(before * oh2, axis=-1, keepdims=True)
    run = run_ref[...] + jnp.sum(both, axis=0, keepdims=True)
    run_ref[...] = run
    cnt_ref[...] = jnp.broadcast_to(run, cnt_ref.shape)
    info = jnp.where(lane == 0, i1, jnp.where(lane == 1, i2, jnp.where(lane == 2, r1, jnp.where(
        lane == 3, r2, jnp.where(lane == 4, gate1, jnp.where(lane == 5, gate2, 0.0))))))
    info_ref[0] = info


def _moe_route(x, gain, sc1p, sh, w_router):
    bn, l, d = x.shape
    tm = min(256, l)
    wr = jnp.pad(w_router, ((0, 0), (0, LANES - w_router.shape[1])))
    vec = pl.BlockSpec((1, 1, d), lambda b, i: (b, 0, 0))
    return pl.pallas_call(
        functools.partial(_route_kernel, tm=tm),
        grid=(bn, l // tm),
        in_specs=[pl.BlockSpec((1, tm, d), lambda b, i: (b, i, 0)),
                  pl.BlockSpec((1, d), lambda b, i: (0, 0)), vec, vec,
                  pl.BlockSpec((d, LANES), lambda b, i: (0, 0))],
        out_specs=[pl.BlockSpec((1, tm, d), lambda b, i: (b, i, 0)),
                   pl.BlockSpec((1, tm, LANES), lambda b, i: (b, i, 0)),
                   pl.BlockSpec((8, LANES), lambda b, i: (0, 0))],
        out_shape=[jax.ShapeDtypeStruct((bn, l, d), F32),
                   jax.ShapeDtypeStruct((bn, l, LANES), F32),
                   jax.ShapeDtypeStruct((8, LANES), F32)],
        scratch_shapes=[pltpu.VMEM((1, LANES), F32)],
        compiler_params=_cp(("arbitrary", "arbitrary")),
        name="moe_route",
    )(x, gain.reshape(1, d), sc1p, sh, wr)


def _row_copy(src, dst, s, t, sem):
    return pltpu.make_async_copy(src.at[pl.ds(s, 1)], dst.at[pl.ds(t, 1)], sem)


def _dispatch_kernel(dest_ref, tok_ref, xb_in_ref, xb_ref, sem, *, tb):
    del xb_in_ref
    base = pl.program_id(0) * tb

    def start(t, _):
        _row_copy(tok_ref, xb_ref, base + t, dest_ref[0, 0, t], sem).start()
        _row_copy(tok_ref, xb_ref, base + t, dest_ref[0, 1, t], sem).start()
        return 0

    lax.fori_loop(0, tb, start, 0)

    def wait(t, _):
        _row_copy(tok_ref, xb_ref, base + t, dest_ref[0, 0, t], sem).wait()
        _row_copy(tok_ref, xb_ref, base + t, dest_ref[0, 1, t], sem).wait()
        return 0

    lax.fori_loop(0, tb, wait, 0)


def _moe_dispatch(tok, dest, n_slots):
    t, d = tok.shape
    nb, _, tb = dest.shape
    xb0 = jnp.zeros((n_slots, d), F32)
    return pl.pallas_call(
        functools.partial(_dispatch_kernel, tb=tb),
        grid=(nb,),
        in_specs=[pl.BlockSpec((1, 2, tb), lambda i: (i, 0, 0), memory_space=pltpu.SMEM),
                  pl.BlockSpec(memory_space=pl.ANY),
                  pl.BlockSpec(memory_space=pl.ANY)],
        out_specs=pl.BlockSpec(memory_space=pl.ANY),
        out_shape=jax.ShapeDtypeStruct((n_slots, d), F32),
        scratch_shapes=[pltpu.SemaphoreType.DMA(())],
        input_output_aliases={2: 0},
        compiler_params=_cp(("arbitrary",)),
        name="moe_dispatch",
    )(dest, tok, xb0)


def _exp_up_kernel(be_ref, x_ref, wg_ref, wu_ref, o_ref, xs_ref):
    del be_ref

    @pl.when(pl.program_id(1) == 0)
    def _():
        xs_ref[...] = x_ref[...].astype(BF16)

    x = xs_ref[...]
    o_ref[...] = (_silu(_dot(x, wg_ref[...])) * _dot(x, wu_ref[...])).astype(o_ref.dtype)


def _moe_up(xb, block_e, w_gate, w_up, blk):
    n_slots, d = xb.shape
    ff = w_gate.shape[2]
    tn = 512
    wspec = pl.BlockSpec((None, d, tn), lambda i, j, be: (be[i], 0, j))
    grid_spec = pltpu.PrefetchScalarGridSpec(
        num_scalar_prefetch=1,
        grid=(n_slots // blk, ff // tn),
        in_specs=[pl.BlockSpec((blk, d), lambda i, j, be: (i, 0)), wspec, wspec],
        out_specs=pl.BlockSpec((blk, tn), lambda i, j, be: (i, j)),
        scratch_shapes=[pltpu.VMEM((blk, d), BF16)],
    )
    return pl.pallas_call(
        _exp_up_kernel,
        grid_spec=grid_spec,
        out_shape=jax.ShapeDtypeStruct((n_slots, ff), BF16),
        compiler_params=_cp(("parallel", "arbitrary")),
        name="moe_expert_up",
    )(block_e, xb, w_gate, w_up)


def _exp_down_kernel(be_ref, h_ref, w_ref, o_ref):
    del be_ref
    o_ref[...] = _dot(h_ref[...], w_ref[...])


def _moe_down(hmid, block_e, w_down, blk):
    n_slots, ff = hmid.shape
    d = w_down.shape[2]
    grid_spec = pltpu.PrefetchScalarGridSpec(
        num_scalar_prefetch=1,
        grid=(n_slots // blk,),
        in_specs=[pl.BlockSpec((blk, ff), lambda i, be: (i, 0)),
                  pl.BlockSpec((None, ff, d), lambda i, be: (be[i], 0, 0))],
        out_specs=pl.BlockSpec((blk, d), lambda i, be: (i, 0)),
    )
    return pl.pallas_call(
        _exp_down_kernel,
        grid_spec=grid_spec,
        out_shape=jax.ShapeDtypeStruct((n_slots, d), F32),
        compiler_params=_cp(("arbitrary",)),
        name="moe_expert_down",
    )(block_e, hmid, w_down)


def _combine_kernel(dest_ref, yb_ref, info_ref, x_ref, g_ref, o_ref, y_ref, sem, *, tb):
    def start(t, _):
        _row_copy(yb_ref, y_ref.at[0], dest_ref[0, 0, 0, t], t, sem).start()
        _row_copy(yb_ref, y_ref.at[1], dest_ref[0, 0, 1, t], t, sem).start()
        return 0

    lax.fori_loop(0, tb, start, 0)

    def wait(t, _):
        _row_copy(yb_ref, y_ref.at[0], dest_ref[0, 0, 0, t], t, sem).wait()
        _row_copy(yb_ref, y_ref.at[1], dest_ref[0, 0, 1, t], t, sem).wait()
        return 0

    lax.fori_loop(0, tb, wait, 0)
    info = info_ref[0]
    lane = lax.broadcasted_iota(jnp.int32, info.shape, 1)
    gate1 = jnp.sum(jnp.where(lane == 4, info, 0.0), axis=-1, keepdims=True)
    gate2 = jnp.sum(jnp.where(lane == 5, info, 0.0), axis=-1, keepdims=True)
    o_ref[0] = x_ref[0] + g_ref[0] * (gate1 * y_ref[0] + gate2 * y_ref[1])


def _moe_combine(yb, dest, info, x, gate):
    bn, l, d = x.shape
    tb = dest.shape[-1]
    nb = l // tb
    dest4 = dest.reshape(bn, nb, 2, tb)
    return pl.pallas_call(
        functools.partial(_combine_kernel, tb=tb),
        grid=(bn, nb),
        in_specs=[pl.BlockSpec((1, 1, 2, tb), lambda b, i: (b, i, 0, 0), memory_space=pltpu.SMEM),
                  pl.BlockSpec(memory_space=pl.ANY),
                  pl.BlockSpec((1, tb, LANES), lambda b, i: (b, i, 0)),
                  pl.BlockSpec((1, tb, d), lambda b, i: (b, i, 0)),
                  pl.BlockSpec((1, 1, d), lambda b, i: (b, 0, 0))],
        out_specs=pl.BlockSpec((1, tb, d), lambda b, i: (b, i, 0)),
        out_shape=jax.ShapeDtypeStruct(x.shape, F32),
        scratch_shapes=[pltpu.VMEM((2, tb, d), F32), pltpu.SemaphoreType.DMA(())],
        input_output_aliases={3: 0},
        compiler_params=_cp(("arbitrary", "arbitrary")),
        name="moe_combine",
    )(dest4, yb, info, x, gate)


def _moe_layer(x, gain, sc1p, sh, gate, w_router, w_gate, w_up, w_down, blk):
    bn, l, d = x.shape
    t = bn * l
    tok, info, cnt = _moe_route(x, gain, sc1p, sh, w_router)
    counts = cnt[0, :N_EXPERTS].astype(jnp.int32)
    padded = (counts + blk - 1) // blk * blk
    pad_end = jnp.cumsum(padded)
    pad_start = pad_end - padded
    n_blocks = -(-(2 * t) // blk) + N_EXPERTS
    block_e = jnp.minimum(
        jnp.sum((jnp.arange(n_blocks, dtype=jnp.int32)[:, None] * blk >= pad_end[None, :]).astype(jnp.int32), axis=1),
        N_EXPERTS - 1).astype(jnp.int32)
    info2 = info.reshape(t, LANES)
    ee = jnp.arange(N_EXPERTS, dtype=jnp.int32)[None, :]

    def slot(e_col, r_col):
        e = info2[:, e_col].astype(jnp.int32)
        return jnp.sum(jnp.where(e[:, None] == ee, pad_start[None, :], 0), axis=1) + info2[:, r_col].astype(jnp.int32)

    tb = min(256, l)
    dest = jnp.stack([slot(0, 2), slot(1, 3)], axis=0).reshape(2, t // tb, tb).transpose(1, 0, 2)
    xb = _moe_dispatch(tok.reshape(t, d), dest, n_blocks * blk)
    hmid = _moe_up(xb, block_e, w_gate, w_up, blk)
    yb = _moe_down(hmid, block_e, w_down, blk)
    return _moe_combine(yb, dest, info, x, gate)


def kernel(x, c, ctx, c_ctx, ada_w, ada_b, norm1_g, norm2_g, normf_g, cf_w_pw1, cf_b_pw1, cf_w_dw, cf_b_dw, cf_ln_g, cf_ln_b, cf_w_pw2, cf_b_pw2, hy_w_in, hy_b_in, hy_w_short, hy_b_short, hy_w_f1, hy_b_f1, hy_freq, hy_w_f2, hy_b_f2, hy_w_f3, hy_bias, hy_w_out, hy_b_out, hg_lb_logits, hg_w_in, hg_gn_g, hg_w_out, ffn_w_gate, ffn_w_up, ffn_w_down, moe_w_router, moe_w_gate, moe_w_up, moe_w_down):
    bn, s, d = x.shape
    cl = ctx.shape[1]
    depth = ada_w.shape[0]
    bf = lambda a: a.astype(BF16)

    cvecs = jnp.concatenate([c, c_ctx[None], jnp.zeros((8 - bn - 1, d), F32)], axis=0)
    mods = _ada_mod(cvecs, ada_w, ada_b)

    def mod_slices(i, ctx_rows):
        m = jnp.broadcast_to(mods[i, bn:bn + 1], (bn, N_MOD * d)) if ctx_rows else mods[i, :bn]
        parts = [m[:, k * d:(k + 1) * d].reshape(bn, 1, d) for k in range(N_MOD)]
        sh1, sc1, g1, sh2, sc2, g2 = parts
        return sh1, 1.0 + sc1, g1, sh2, 1.0 + sc2, g2

    p = jax.nn.softmax(hg_lb_logits.astype(F32), axis=0)
    lb_all = jnp.cumsum(p, axis=0) - p[0]

    x = _embed(x)
    xc = ctx

    for i in range(depth):
        last = i == depth - 1
        kind, j = i % 3, i // 3
        streams = [(x, mod_slices(i, False), False)]
        if not last:
            streams.append((xc, mod_slices(i, True), True))

        new = []
        if kind == 0:
            wa, wg = bf(cf_w_pw1[j][:, :d]), bf(cf_w_pw1[j][:, d:])
            ba, bg = cf_b_pw1[j][:d], cf_b_pw1[j][d:]
            w2 = bf(cf_w_pw2[j])
            for xs, (sh1, sc1, g1, _, _, _), _ in streams:
                u = _nmm(xs, norm1_g[i], sc1, sh1, [wa, wg], [ba, bg], epi="glu")
                new.append(_conformer_tail(u, cf_w_dw[j], cf_b_dw[j], cf_ln_g[j], cf_ln_b[j], w2, cf_b_pw2[j], xs, g1))
        elif kind == 1:
            order = hy_bias.shape[1]
            w_in_t = bf(hy_w_in[j].T)
            w_out = bf(hy_w_out[j])
            for xs, (sh1, sc1, g1, _, _, _), _ in streams:
                l = xs.shape[1]
                n2, r, n1, lp = _fft_plan(l)
                consts = _fft_consts(n1, n2)
                filt = _hyena_filters_t(l, lp, hy_w_f1[j], hy_b_f1[j], hy_freq[j], hy_w_f2[j], hy_b_f2[j],
                                        hy_w_f3[j], order, d)
                spec = _hyena_spectrum(filt, consts, n1, n2)
                ut = _nmm(xs, norm1_g[i], sc1, sh1, [w_in_t], [hy_b_in[j]], transposed=True)
                if lp > l:
                    ut = jnp.pad(ut, ((0, 0), (0, 0), (0, lp - l)))
                zt = _hyena_conv(ut, hy_w_short[j], hy_b_short[j], hy_bias[j], spec, consts, n1, n2, l)
                new.append(_mmres(zt, w_out, hy_b_out[j], xs, g1, trans_a=True))
        else:
            w_in = bf(hg_w_in[j])
            w_out = bf(hg_w_out[j])
            nh = d // HG_HEAD
            state = jnp.zeros((bn, 2, nh, HG_HEAD, HG_HEAD), F32)
            outs = {}
            for xs, (sh1, sc1, g1, _, _, _), is_ctx in ((xc, mod_slices(i, True), True), streams[0]):
                u5 = _nmm(xs, norm1_g[i], sc1, sh1, [w_in], [jnp.zeros((5 * d,), F32)])
                o_f, o_b, state = _hgrn_scan(u5, lb_all[i], state)
                if not (is_ctx and last):
                    outs[is_ctx] = _hgrn_out(o_f, o_b, u5, hg_gn_g[j], w_out, xs, g1)
            new = [outs[False]] + ([outs[True]] if not last else [])

        x = new[0]
        if not last:
            xc = new[1]
        streams = [(x, mod_slices(i, False), False)]
        if not last:
            streams.append((xc, mod_slices(i, True), True))

        new = []
        if i % 2 == 0:
            wg, wu, wd = bf(ffn_w_gate[i // 2]), bf(ffn_w_up[i // 2]), bf(ffn_w_down[i // 2])
            zb = jnp.zeros((wg.shape[1],), F32)
            for xs, (_, _, _, sh2, sc2, g2), _ in streams:
                hmid = _nmm(xs, norm2_g[i], sc2, sh2, [wg, wu], [zb, zb], epi="swiglu", out_dtype=BF16)
                new.append(_mmres(hmid, wd, jnp.zeros((d,), F32), xs, g2))
        else:
            wg, wu, wd = bf(moe_w_gate[i // 2]), bf(moe_w_up[i // 2]), bf(moe_w_down[i // 2])
            for xs, (_, _, _, sh2, sc2, g2), is_ctx in streams:
                new.append(_moe_layer(xs, norm2_g[i], sc2, sh2, g2, moe_w_router[i // 2], wg, wu, wd,
                                      blk=128 if is_ctx else 512))
        x = new[0]
        if not last:
            xc = new[1]

    return _final_norm(x, normf_g)


def _fnorm_kernel(x_ref, g_ref, o_ref):
    xv = x_ref[0]
    o_ref[0] = xv * lax.rsqrt(jnp.mean(xv * xv, axis=-1, keepdims=True) + EPS) * g_ref[...]


def _final_norm(x, g):
    bn, l, d = x.shape
    tm = min(512, l)
    return pl.pallas_call(
        _fnorm_kernel,
        grid=(bn, l // tm),
        in_specs=[pl.BlockSpec((1, tm, d), lambda b, i: (b, i, 0)), pl.BlockSpec((1, d), lambda b, i: (0, 0))],
        out_specs=pl.BlockSpec((1, tm, d), lambda b, i: (b, i, 0)),
        out_shape=jax.ShapeDtypeStruct(x.shape, F32),
        compiler_params=_cp(("parallel", "parallel")),
        name="final_norm",
    )(x, g.reshape(1, d))
```

```python
import functools
import math

import numpy as np
import jax
import jax.numpy as jnp
from jax import lax
from jax.experimental import pallas as pl
from jax.experimental.pallas import tpu as pltpu

F32 = jnp.float32
BF16 = jnp.bfloat16
EPS = 1e-6

GRID_W = 64
N_MOD = 6
CONV_WIDTH = 31
CONV_HALO = 16
HY_EMB = 33
HY_EMB_PAD = 40
HY_TARGET = 1e-2
HY_DECAY_MIN = math.log(HY_TARGET) / 1.5
HY_DECAY_MAX = math.log(HY_TARGET) / 0.3
HG_HEAD = 128
HG_CHUNK = 64
N_EXPERTS = 8
LANES = 128
VMEM_LIMIT = 56 * 1024 * 1024


def _cp(sem, vmem=VMEM_LIMIT):
    return pltpu.CompilerParams(dimension_semantics=sem, vmem_limit_bytes=vmem)


def _dot(a, b):
    return jnp.dot(a, b, preferred_element_type=F32)


def _dot_nt(a, b):
    return lax.dot_general(a, b, (((1,), (1,)), ((), ())), preferred_element_type=F32)


def _dot_tn(a, b):
    return lax.dot_general(a, b, (((0,), (0,)), ((), ())), preferred_element_type=F32)


def _split3(a):
    h = a.astype(BF16)
    r = a - h.astype(F32)
    m = r.astype(BF16)
    l = (r - m.astype(F32)).astype(BF16)
    return h, m, l


def _dot_acc(a, b, dot=_dot):
    ah, am, _ = _split3(a)
    bh, bm, _ = _split3(b)
    return dot(ah, bh) + (dot(ah, bm) + dot(am, bh))


def _silu(x):
    return x * jax.nn.sigmoid(x)


def _ada_kernel(c_ref, w_ref, b_ref, o_ref):
    s = _silu(c_ref[...])
    o_ref[0] = _dot_acc(s, w_ref[0]) + b_ref[0]


def _ada_mod(cvecs, ada_w, ada_b):
    depth, d, n = ada_w.shape
    tn = 1024
    return pl.pallas_call(
        _ada_kernel,
        grid=(depth, n // tn),
        in_specs=[pl.BlockSpec((8, d), lambda i, j: (0, 0)),
                  pl.BlockSpec((1, d, tn), lambda i, j: (i, 0, j)),
                  pl.BlockSpec((1, 1, tn), lambda i, j: (i, 0, j))],
        out_specs=pl.BlockSpec((1, 8, tn), lambda i, j: (i, 0, j)),
        out_shape=jax.ShapeDtypeStruct((depth, 8, n), F32),
        compiler_params=_cp(("parallel", "parallel")),
        name="ada_mod",
    )(cvecs, ada_w, ada_b.reshape(depth, 1, n))


def _embed_kernel(x_ref, om_ref, o_ref, pos_ref, *, tm):
    i = pl.program_id(0)

    @pl.when(pl.program_id(1) == 0)
    def _():
        q = om_ref.shape[1]
        t = (lax.broadcasted_iota(jnp.int32, (tm, 1), 0) + i * tm).astype(F32)
        r = jnp.floor(t / GRID_W)
        col = t - r * GRID_W
        a = r * om_ref[...]
        b = col * om_ref[...]
        pos_ref[:, 0 * q:1 * q] = jnp.sin(a)
        pos_ref[:, 1 * q:2 * q] = jnp.cos(a)
        pos_ref[:, 2 * q:3 * q] = jnp.sin(b)
        pos_ref[:, 3 * q:4 * q] = jnp.cos(b)

    o_ref[0] = x_ref[0] + pos_ref[...]


def _embed(x):
    bn, s, d = x.shape
    tm = min(512, s)
    q = d // 4
    omega = (1.0 / (10000.0 ** (jnp.arange(q, dtype=F32) / q))).reshape(1, q)
    return pl.pallas_call(
        functools.partial(_embed_kernel, tm=tm),
        grid=(s // tm, bn),
        in_specs=[pl.BlockSpec((1, tm, d), lambda i, b: (b, i, 0)),
                  pl.BlockSpec((1, q), lambda i, b: (0, 0))],
        out_specs=pl.BlockSpec((1, tm, d), lambda i, b: (b, i, 0)),
        out_shape=jax.ShapeDtypeStruct(x.shape, F32),
        scratch_shapes=[pltpu.VMEM((tm, d), F32)],
        compiler_params=_cp(("parallel", "arbitrary")),
        name="pos_embed",
    )(x, omega)


def _norm_mod(x, g, sc1p, sh):
    ms = jnp.mean(x * x, axis=-1, keepdims=True)
    return (x * lax.rsqrt(ms + EPS) * g) * sc1p + sh


def _nmm_kernel(*refs, nw, epi, transposed):
    x_ref, g_ref, sc_ref, sh_ref = refs[:4]
    w_refs = refs[4:4 + nw]
    b_refs = refs[4 + nw:4 + 2 * nw]
    o_ref = refs[4 + 2 * nw]
    hs_ref = refs[5 + 2 * nw]

    @pl.when(pl.program_id(2) == 0)
    def _():
        hs_ref[...] = _norm_mod(x_ref[0], g_ref[...], sc_ref[0], sh_ref[0]).astype(BF16)

    h = hs_ref[...]
    if transposed:
        r = _dot_nt(w_refs[0][...], h) + b_refs[0][...]
    else:
        outs = [_dot(h, w[...]) + b[...] for w, b in zip(w_refs, b_refs)]
        if epi == "plain":
            r = outs[0]
        elif epi == "glu":
            r = outs[0] * jax.nn.sigmoid(outs[1])
        else:
            r = _silu(outs[0]) * outs[1]
    o_ref[0] = r.astype(o_ref.dtype)


def _col_tile(n, target=1024):
    best = None
    for t in range(256, n + 1, 256):
        if n % t == 0 and t <= target * 7 // 4:
            best = t
    return best if best is not None else n


def _nmm(x, gain, sc1p, sh, ws, bs, *, epi="plain", out_dtype=F32, transposed=False):
    bn, l, d = x.shape
    n = ws[0].shape[0] if transposed else ws[0].shape[1]
    tn = _col_tile(n)
    tm = min(1024 if len(ws) * tn <= 2048 else 512, l)
    nw = len(ws)
    if transposed:
        w_specs = [pl.BlockSpec((tn, d), lambda b, i, j: (j, 0))]
        b_specs = [pl.BlockSpec((tn, 1), lambda b, i, j: (j, 0))]
        bs = [bs[0].reshape(n, 1)]
        out_spec = pl.BlockSpec((1, tn, tm), lambda b, i, j: (b, j, i))
        out_shape = jax.ShapeDtypeStruct((bn, n, l), out_dtype)
    else:
        w_specs = [pl.BlockSpec((d, tn), lambda b, i, j: (0, j)) for _ in ws]
        b_specs = [pl.BlockSpec((1, tn), lambda b, i, j: (0, j)) for _ in ws]
        bs = [b.reshape(1, n) for b in bs]
        out_spec = pl.BlockSpec((1, tm, tn), lambda b, i, j: (b, i, j))
        out_shape = jax.ShapeDtypeStruct((bn, l, n), out_dtype)
    vec = pl.BlockSpec((1, 1, d), lambda b, i, j: (b, 0, 0))
    return pl.pallas_call(
        functools.partial(_nmm_kernel, nw=nw, epi=epi, transposed=transposed),
        grid=(bn, l // tm, n // tn),
        in_specs=[pl.BlockSpec((1, tm, d), lambda b, i, j: (b, i, 0)),
                  pl.BlockSpec((1, d), lambda b, i, j: (0, 0)), vec, vec] + w_specs + b_specs,
        out_specs=out_spec,
        out_shape=out_shape,
        scratch_shapes=[pltpu.VMEM((tm, d), BF16)],
        compiler_params=_cp(("parallel", "parallel", "arbitrary")),
        name="norm_mod_matmul_" + ("t" if transposed else epi),
    )(x, gain.reshape(1, d), sc1p, sh, *ws, *bs)


def _mmres_kernel(a_ref, w_ref, b_ref, x_ref, g_ref, o_ref, *, trans_a):
    a = a_ref[0]
    y = (_dot_tn(a, w_ref[...]) if trans_a else _dot(a, w_ref[...])) + b_ref[...]
    o_ref[0] = x_ref[0] + g_ref[0] * y


def _mmres(a, w, bias, x, gate, *, trans_a=False):
    bn, l, n = x.shape
    k = w.shape[0]
    tm = min(512, l)
    a_spec = (pl.BlockSpec((1, k, tm), lambda b, i: (b, 0, i)) if trans_a
              else pl.BlockSpec((1, tm, k), lambda b, i: (b, i, 0)))
    return pl.pallas_call(
        functools.partial(_mmres_kernel, trans_a=trans_a),
        grid=(bn, l // tm),
        in_specs=[a_spec,
                  pl.BlockSpec((k, n), lambda b, i: (0, 0)),
                  pl.BlockSpec((1, n), lambda b, i: (0, 0)),
                  pl.BlockSpec((1, tm, n), lambda b, i: (b, i, 0)),
                  pl.BlockSpec((1, 1, n), lambda b, i: (b, 0, 0))],
        out_specs=pl.BlockSpec((1, tm, n), lambda b, i: (b, i, 0)),
        out_shape=jax.ShapeDtypeStruct(x.shape, F32),
        input_output_aliases={3: 0},
        compiler_params=_cp(("parallel", "parallel")),
        name="matmul_residual" + ("_ta" if trans_a else ""),
    )(a, w, bias.reshape(1, n), x, gate)


def _cf2_kernel(prev_ref, cur_ref, next_ref, wdw_ref, bdw_ref, lng_ref, lnb_ref, w2_ref, b2_ref,
                x_ref, g_ref, o_ref, buf_ref, sh_ref, cv_ref, *, tm):
    i = pl.program_id(1)
    nt = pl.num_programs(1)
    h = CONV_HALO
    buf_ref[0:h] = jnp.where(i > 0, prev_ref[0], 0.0)
    buf_ref[h:h + tm] = cur_ref[0]
    buf_ref[h + tm:2 * h + tm] = jnp.where(i < nt - 1, next_ref[0], 0.0)
    span = sh_ref.shape[1]
    for s in range(8):
        sh_ref[s] = buf_ref[s:s + span]
    off = h - (CONV_WIDTH - 1) // 2
    rc = 8
    for c in range(tm // rc):
        acc = None
        for k in range(CONV_WIDTH):
            q, s = divmod(off + k, 8)
            term = sh_ref[s, c * rc + 8 * q:c * rc + 8 * q + rc] * wdw_ref[k:k + 1]
            acc = term if acc is None else acc + term
        cv_ref[c * rc:(c + 1) * rc] = acc + bdw_ref[...]
    u = cv_ref[...]
    mu = jnp.mean(u, axis=-1, keepdims=True)
    uc = u - mu
    var = jnp.mean(uc * uc, axis=-1, keepdims=True)
    v = _silu(uc * lax.rsqrt(var + EPS) * lng_ref[...] + lnb_ref[...])
    y = _dot(v.astype(BF16), w2_ref[...]) + b2_ref[...]
    o_ref[0] = x_ref[0] + g_ref[0] * y


def _conformer_tail(u, w_dw, b_dw, ln_g, ln_b, w2, b2, x, gate):
    bn, l, d = x.shape
    tm = min(256, l)
    h = CONV_HALO
    nh = l // h
    r = tm // h
    row = lambda a: a.reshape(1, d)
    full = lambda shp: pl.BlockSpec(shp, lambda b, i: (0,) * len(shp))
    return pl.pallas_call(
        functools.partial(_cf2_kernel, tm=tm),
        grid=(bn, l // tm),
        in_specs=[pl.BlockSpec((1, h, d), lambda b, i: (b, jnp.maximum(i * r - 1, 0), 0)),
                  pl.BlockSpec((1, tm, d), lambda b, i: (b, i, 0)),
                  pl.BlockSpec((1, h, d), lambda b, i: (b, jnp.minimum((i + 1) * r, nh - 1), 0)),
                  full((CONV_WIDTH, d)), full((1, d)), full((1, d)), full((1, d)),
                  full((d, d)), full((1, d)),
                  pl.BlockSpec((1, tm, d), lambda b, i: (b, i, 0)),
                  pl.BlockSpec((1, 1, d), lambda b, i: (b, 0, 0))],
        out_specs=pl.BlockSpec((1, tm, d), lambda b, i: (b, i, 0)),
        out_shape=jax.ShapeDtypeStruct(x.shape, F32),
        scratch_shapes=[pltpu.VMEM((tm + 2 * h, d), F32),
                        pltpu.VMEM((8, tm + 2 * h - 8, d), F32),
                        pltpu.VMEM((tm, d), F32)],
        input_output_aliases={9: 0},
        compiler_params=_cp(("parallel", "parallel")),
        name="conformer_tail",
    )(u, u, u, w_dw, row(b_dw), row(ln_g), row(ln_b), w2, row(b2), x, gate)


def _fft_plan(l):
    n2 = 256 if l >= 4096 else 128
    r = max(16, -(-l // n2))
    return n2, r, 2 * r, r * n2


def _fft_consts(n1, n2):
    r = n1 // 2
    n = n1 * n2
    k1 = np.arange(n1, dtype=np.float64)[:, None]
    a1 = 2.0 * np.pi * k1 * np.arange(r, dtype=np.float64)[None, :] / n1
    fc1 = np.concatenate([np.cos(a1), -np.sin(a1)], axis=0)
    at = 2.0 * np.pi * k1 * np.arange(n2, dtype=np.float64)[None, :] / n
    tw = np.stack([np.cos(at), -np.sin(at)], axis=0)
    a2 = 2.0 * np.pi * np.outer(np.arange(n2, dtype=np.float64), np.arange(n2, dtype=np.float64)) / n2
    f2 = np.concatenate([np.cos(a2), -np.sin(a2)], axis=1)
    g2 = np.concatenate([np.cos(a2), np.sin(a2)], axis=1)
    ai = 2.0 * np.pi * np.arange(r, dtype=np.float64)[:, None] * np.arange(n1, dtype=np.float64)[None, :] / n1
    c1i = np.concatenate([np.cos(ai), -np.sin(ai)], axis=1) / n
    return (jnp.asarray(fc1, BF16), jnp.asarray(tw, F32), jnp.asarray(f2, BF16),
            jnp.asarray(g2, BF16), jnp.asarray(c1i, BF16))


def _fft_fwd_rows(z, fc1, tr, ti):
    n1 = tr.shape[0]
    a = _dot(fc1, z.astype(BF16))
    ar, ai = a[:n1], a[n1:]
    return ar * tr - ai * ti, ar * ti + ai * tr


def _cplx_rows(p, ch, n1, n2):
    p4 = p.reshape(ch, 2, n1, 2 * n2)
    return p4[:, 0, :, :n2] - p4[:, 1, :, n2:], p4[:, 0, :, n2:] + p4[:, 1, :, :n2]


def _hyfilt_kernel(w1_ref, b1_ref, fr_ref, w2_ref, b2_ref, fb_ref, w3_ref, dl_ref, o_ref, hid_ref, *, l, lp):
    lane = lax.broadcasted_iota(jnp.int32, (1, lp), 1)
    pos = lane.astype(F32)
    t = pos / (l - 1)

    @pl.when((pl.program_id(0) == 0) & (pl.program_id(1) == 0))
    def _():
        w = (2.0 * math.pi) * pos / l
        arg = fb_ref[...] * w
        row = lax.broadcasted_iota(jnp.int32, (HY_EMB_PAD, 1), 0)
        bands = (HY_EMB - 1) // 2
        z = jnp.where(row == 0, t,
                      jnp.where(row <= bands, jnp.cos(arg),
                                jnp.where(row < HY_EMB, -jnp.sin(arg), 0.0)))
        fr = fr_ref[...]
        hid = jnp.sin(fr * (_dot_acc(w1_ref[...], z) + b1_ref[...]))
        for j in range(w2_ref.shape[0]):
            hid = jnp.sin(fr * (_dot_acc(w2_ref[j], hid) + b2_ref[j]))
        hid_ref[...] = hid

    decay = jnp.exp(-t * dl_ref[...])
    hid = hid_ref[...]
    valid = lane < l
    f0 = jnp.where(valid, _dot_acc(w3_ref[0, 0], hid) * decay, 0.0)
    f1 = jnp.where(valid & (lane > 0), _dot_acc(w3_ref[0, 1], hid) * decay, 0.0)
    norm = jnp.sum(jnp.abs(f0), axis=-1, keepdims=True) + jnp.sum(jnp.abs(f1), axis=-1, keepdims=True)
    o_ref[0, 0] = f0 / norm
    o_ref[0, 1] = f1 / norm


def _hyena_filters_t(l, lp, w_f1, b_f1, freq, w_f2, b_f2, w_f3, order, d):
    fw = w_f1.shape[1]
    bands = (HY_EMB - 1) // 2
    fb = jnp.linspace(1e-4, bands - 1, bands, dtype=F32)
    fbcol = jnp.concatenate([jnp.zeros((1,), F32), fb, fb, jnp.zeros((HY_EMB_PAD - HY_EMB,), F32)]).reshape(-1, 1)
    w1t = jnp.pad(w_f1.T, ((0, 0), (0, HY_EMB_PAD - HY_EMB)))
    w2t = jnp.swapaxes(w_f2, 1, 2)
    w3t = w_f3.T.reshape(order, 2, d, fw)
    deltas = jnp.abs(jnp.linspace(HY_DECAY_MIN, HY_DECAY_MAX, d, dtype=F32)).reshape(d, 1)
    tr = 32
    c2 = lambda shp: pl.BlockSpec(shp, lambda o, c: (0,) * len(shp))
    return pl.pallas_call(
        functools.partial(_hyfilt_kernel, l=l, lp=lp),
        grid=(order, d // tr),
        in_specs=[c2((fw, HY_EMB_PAD)), c2((fw, 1)), c2((fw, 1)), c2(w2t.shape), c2((w_f2.shape[0], fw, 1)),
                  c2((HY_EMB_PAD, 1)),
                  pl.BlockSpec((1, 2, tr, fw), lambda o, c: (o, 0, c, 0)),
                  pl.BlockSpec((tr, 1), lambda o, c: (c, 0))],
        out_specs=pl.BlockSpec((1, 2, tr, lp), lambda o, c: (o, 0, c, 0)),
        out_shape=jax.ShapeDtypeStruct((order, 2, d, lp), F32),
        scratch_shapes=[pltpu.VMEM((fw, lp), F32)],
        compiler_params=_cp(("arbitrary", "arbitrary")),
        name="hyena_filter",
    )(w1t, b_f1.reshape(fw, 1), freq.reshape(fw, 1), w2t, b_f2.reshape(-1, fw, 1), fbcol, w3t, deltas)


def _hyspec_kernel(f_ref, fc1_ref, tw_ref, f2_ref, o_ref, ys_ref, *, dc, n1, n2):
    tr, ti = tw_ref[0], tw_ref[1]
    fc1 = fc1_ref[...]

    def stage1(c, _):
        for s in range(2):
            yr, yi = _fft_fwd_rows(f_ref[0, s, c], fc1, tr, ti)
            ys_ref[s, c, :n1] = yr.astype(BF16)
            ys_ref[s, c, n1:] = yi.astype(BF16)
        return 0

    for c in range(dc):
        stage1(c, 0)
    zs = []
    for s in range(2):
        p = _dot(ys_ref[s].reshape(dc * 2 * n1, n2), f2_ref[...])
        zs.append(_cplx_rows(p, dc, n1, n2))
    o_ref[0, 0] = zs[0][0] + zs[1][0]
    o_ref[0, 1] = zs[0][1] - zs[1][1]


def _hyena_spectrum(filt, consts, n1, n2):
    order, _, d, lp = filt.shape
    r = n1 // 2
    dc = 4
    fc1, tw, f2, _, _ = consts
    c2 = lambda a: pl.BlockSpec(a.shape, lambda o, c: (0,) * a.ndim)
    return pl.pallas_call(
        functools.partial(_hyspec_kernel, dc=dc, n1=n1, n2=n2),
        grid=(order, d // dc),
        in_specs=[pl.BlockSpec((1, 2, dc, r, n2), lambda o, c: (o, 0, c, 0, 0)), c2(fc1), c2(tw), c2(f2)],
        out_specs=pl.BlockSpec((1, 2, dc, n1, n2), lambda o, c: (o, 0, c, 0, 0)),
        out_shape=jax.ShapeDtypeStruct((order, 2, d, n1, n2), F32),
        scratch_shapes=[pltpu.VMEM((2, dc, 2 * n1, n2), BF16)],
        compiler_params=_cp(("parallel", "parallel")),
        name="hyena_spectrum",
    )(filt.reshape(order, 2, d, r, n2), fc1, tw, f2)


def _hyconv_kernel(wsh_ref, hb_ref, v_ref, x1_ref, x2_ref, h_ref, fc1_ref, tw_ref, f2_ref, g2_ref, c1i_ref,
                   o_ref, cs_ref, ys_ref, vs_ref, *, dc, ch, r, n1, n2, d, l, order):
    cb = pl.program_id(0)
    tr, ti = tw_ref[0], tw_ref[1]
    row = lax.broadcasted_iota(jnp.int32, (r, n2), 0)
    lane = lax.broadcasted_iota(jnp.int32, (r, n2), 1)
    valid = (row * n2 + lane) < l

    def short_conv(c, _):
        chan = cb * dc + c
        for s, ref in enumerate((v_ref, x1_ref, x2_ref)):
            u = ref[0, 0, c]
            idx = s * d + chan
            a = pltpu.roll(u, 1, axis=1)
            prev = jnp.where(lane == 0, jnp.where(row == 0, 0.0, pltpu.roll(a, 1, axis=0)), a)
            b = pltpu.roll(u, n2 - 1, axis=1)
            nxt = jnp.where(lane == n2 - 1, jnp.where(row == r - 1, 0.0, pltpu.roll(b, r - 1, axis=0)), b)
            y = wsh_ref[0, idx] * prev + wsh_ref[1, idx] * u + wsh_ref[2, idx] * nxt + wsh_ref[3, idx]
            cs_ref[s, c] = jnp.where(valid, y, 0.0)
        return 0

    for c in range(dc):
        short_conv(c, 0)

    for n in range(order):
        def stage1(c, _):
            yr, yi = _fft_fwd_rows(cs_ref[0, c], fc1_ref[...], tr, ti)
            ys_ref[c, :n1] = yr.astype(BF16)
            ys_ref[c, n1:] = yi.astype(BF16)
            return 0

        for c in range(dc):
            stage1(c, 0)

        for c0 in range(0, dc, ch):
            p = _dot(ys_ref[c0:c0 + ch].reshape(ch * 2 * n1, n2), f2_ref[...])
            zr, zi = _cplx_rows(p, ch, n1, n2)
            hr, hi = h_ref[n, 0, c0:c0 + ch], h_ref[n, 1, c0:c0 + ch]
            vs_ref[c0:c0 + ch, :n1] = (zr * hr - zi * hi).astype(BF16)
            vs_ref[c0:c0 + ch, n1:] = (zr * hi + zi * hr).astype(BF16)
            q = _dot(vs_ref[c0:c0 + ch].reshape(ch * 2 * n1, n2), g2_ref[...])
            ur, ui = _cplx_rows(q, ch, n1, n2)
            ys_ref[c0:c0 + ch, :n1] = (ur * tr + ui * ti).astype(BF16)
            ys_ref[c0:c0 + ch, n1:] = (ui * tr - ur * ti).astype(BF16)

        def stage4(c, _):
            chan = cb * dc + c
            conv = _dot(c1i_ref[...], ys_ref[c])
            z = cs_ref[n + 1, c] * (conv + hb_ref[n, chan] * cs_ref[0, c])
            if n == order - 1:
                o_ref[0, c] = z.astype(o_ref.dtype)
            else:
                cs_ref[0, c] = z
            return 0

        for c in range(dc):
            stage4(c, 0)


def _hyena_conv(ut, w_short, b_short, hy_bias, spec, consts, n1, n2, l):
    bn, d3, lp = ut.shape
    d = d3 // 3
    order = hy_bias.shape[0]
    r = n1 // 2
    dc = 8
    ch = min(4, dc)
    fc1, tw, f2, g2, c1i = consts
    wsh = jnp.concatenate([w_short, b_short.reshape(1, d3)], axis=0)
    u5 = ut.reshape(bn, 3, d, r, n2)
    cst = lambda a: pl.BlockSpec(a.shape, lambda c, b, *_: (0,) * a.ndim)
    stream = lambda s: pl.BlockSpec((1, 1, dc, r, n2), lambda c, b, *_: (b, s, c, 0, 0))
    grid_spec = pltpu.PrefetchScalarGridSpec(
        num_scalar_prefetch=2,
        grid=(d // dc, bn),
        in_specs=[stream(0), stream(1), stream(2),
                  pl.BlockSpec((order, 2, dc, n1, n2), lambda c, b, *_: (0, 0, c, 0, 0)),
                  cst(fc1), cst(tw), cst(f2), cst(g2), cst(c1i)],
        out_specs=pl.BlockSpec((1, dc, r, n2), lambda c, b, *_: (b, c, 0, 0)),
        scratch_shapes=[pltpu.VMEM((3, dc, r, n2), F32),
                        pltpu.VMEM((dc, 2 * n1, n2), BF16),
                        pltpu.VMEM((dc, 2 * n1, n2), BF16)],
    )
    out = pl.pallas_call(
        functools.partial(_hyconv_kernel, dc=dc, ch=ch, r=r, n1=n1, n2=n2, d=d, l=l, order=order),
        grid_spec=grid_spec,
        out_shape=jax.ShapeDtypeStruct((bn, d, r, n2), BF16),
        compiler_params=_cp(("parallel", "arbitrary")),
        name="hyena_conv",
    )(wsh, hy_bias, u5, u5, u5, spec, fc1, tw, f2, g2, c1i)
    return out.reshape(bn, d, lp)


def _cumsum_acc(tri, g):
    gh, gm, gl = _split3(g)
    return _dot(tri, gh) + (_dot(tri, gm) + _dot(tri, gl))


def _scan_kernel(qf_ref, vf_ref, zf_ref, qb_ref, vb_ref, zb_ref, lb_ref, s0_ref,
                 of_ref, ob_ref, sf_ref, st_ref, *, tt, hp):
    i = pl.program_id(2)
    ck = HG_CHUNK

    @pl.when(i == 0)
    def _():
        st_ref[...] = s0_ref[0]

    rr = lax.broadcasted_iota(jnp.int32, (ck, ck), 0)
    cc = lax.broadcasted_iota(jnp.int32, (ck, ck), 1)
    keeps = (rr >= cc, rr <= cc)
    tris = tuple(kp.astype(BF16) for kp in keeps)

    def chunk(q, v, z, lb, st, di):
        f = lb + (1.0 - lb) * jax.nn.sigmoid(z)
        g = jnp.log(f)
        k = 1.0 - f
        b = _cumsum_acc(tris[di], g)
        btot = b[ck - 1:ck] if di == 0 else b[0:1]
        qd = (_silu(q) * jnp.exp(b)).astype(BF16)
        ki = (k * jnp.exp(-b)).astype(BF16)
        ke = (k * jnp.exp(btot - b)).astype(BF16)
        vb = v.astype(BF16)
        att = jnp.where(keeps[di], _dot_nt(qd, ki), 0.0)
        o = _dot(att.astype(BF16), vb) + _dot_nt(qd, st.astype(BF16))
        return o, st * jnp.exp(btot) + _dot_tn(vb, ke)

    nck = tt // ck
    for h in range(hp):
        hs = slice(h * HG_HEAD, (h + 1) * HG_HEAD)
        lb = lb_ref[:, hs]
        s_f = st_ref[0, h]
        s_b = st_ref[1, h]
        for c in range(nck):
            sl = slice(c * ck, (c + 1) * ck)
            o, s_f = chunk(qf_ref[0, sl, hs], vf_ref[0, sl, hs], zf_ref[0, sl, hs], lb, s_f, 0)
            of_ref[0, sl, hs] = o
            sl = slice((nck - 1 - c) * ck, (nck - c) * ck)
            o, s_b = chunk(qb_ref[0, sl, hs], vb_ref[0, sl, hs], zb_ref[0, sl, hs], lb, s_b, 1)
            ob_ref[0, sl, hs] = o
        st_ref[0, h] = s_f
        st_ref[1, h] = s_b

    @pl.when(i == pl.num_programs(2) - 1)
    def _():
        sf_ref[0] = st_ref[...]


def _hgrn_scan(u5, lb, s0):
    bn, l, d5 = u5.shape
    d = d5 // 5
    nh = d // HG_HEAD
    hp = 4 if nh % 4 == 0 else 1
    hw = hp * HG_HEAD
    ng = nh // hp
    tt = min(256, l)
    nt = l // tt
    col = lambda s, rev: pl.BlockSpec(
        (1, tt, hw), (lambda b, h, i: (b, nt - 1 - i, s * ng + h)) if rev else (lambda b, h, i: (b, i, s * ng + h)))
    st_spec = pl.BlockSpec((1, 2, hp, HG_HEAD, HG_HEAD), lambda b, h, i: (b, 0, h, 0, 0))
    return pl.pallas_call(
        functools.partial(_scan_kernel, tt=tt, hp=hp),
        grid=(bn, ng, nt),
        in_specs=[col(0, False), col(1, False), col(3, False), col(0, True), col(1, True), col(4, True),
                  pl.BlockSpec((1, hw), lambda b, h, i: (0, h)), st_spec],
        out_specs=[col(0, False), col(0, True), st_spec],
        out_shape=[jax.ShapeDtypeStruct((bn, l, d), F32), jax.ShapeDtypeStruct((bn, l, d), F32),
                   jax.ShapeDtypeStruct(s0.shape, F32)],
        scratch_shapes=[pltpu.VMEM((2, hp, HG_HEAD, HG_HEAD), F32)],
        compiler_params=_cp(("parallel", "parallel", "arbitrary")),
        name="hgrn2_scan",
    )(u5, u5, u5, u5, u5, u5, lb.reshape(1, d), s0)


def _hgout_kernel(of_ref, ob_ref, og_ref, gn_ref, w_ref, x_ref, g_ref, o_ref, a_ref, *, nh):
    for h in range(nh):
        sl = slice(h * HG_HEAD, (h + 1) * HG_HEAD)
        o = of_ref[0, :, sl] + ob_ref[0, :, sl]
        o = o * lax.rsqrt(jnp.mean(o * o, axis=-1, keepdims=True) + EPS) * gn_ref[:, sl]
        a_ref[:, sl] = (o * _silu(og_ref[0, :, sl])).astype(BF16)
    o_ref[0] = x_ref[0] + g_ref[0] * _dot(a_ref[...], w_ref[...])


def _hgrn_out(o_f, o_b, u5, gn_g, w_out, x, gate):
    bn, l, d = x.shape
    nh = d // HG_HEAD
    tm = min(512, l)
    blk = lambda c: pl.BlockSpec((1, tm, d), lambda b, i: (b, i, c))
    return pl.pallas_call(
        functools.partial(_hgout_kernel, nh=nh),
        grid=(bn, l // tm),
        in_specs=[blk(0), blk(0), blk(2),
                  pl.BlockSpec((1, d), lambda b, i: (0, 0)),
                  pl.BlockSpec((d, d), lambda b, i: (0, 0)),
                  blk(0), pl.BlockSpec((1, 1, d), lambda b, i: (b, 0, 0))],
        out_specs=blk(0),
        out_shape=jax.ShapeDtypeStruct(x.shape, F32),
        scratch_shapes=[pltpu.VMEM((tm, d), BF16)],
        input_output_aliases={5: 0},
        compiler_params=_cp(("parallel", "parallel")),
        name="hgrn2_out",
    )(o_f, o_b, u5, gn_g.reshape(1, d), w_out, x, gate)


def _route_kernel(x_ref, g_ref, sc_ref, sh_ref, wr_ref, tok_ref, info_ref, cnt_ref, run_ref, *, tm):
    first = (pl.program_id(0) == 0) & (pl.program_id(1) == 0)

    @pl.when(first)
    def _():
        run_ref[...] = jnp.zeros_like(run_ref)

    h = _norm_mod(x_ref[0], g_ref[...], sc_ref[0], sh_ref[0])
    tok_ref[0] = h
    logits = _dot_acc(h, wr_ref[...])
    lane = lax.broadcasted_iota(jnp.int32, (tm, LANES), 1)
    lanef = lane.astype(F32)
    neg = jnp.float32(-jnp.inf)
    lg = jnp.where(lane < N_EXPERTS, logits, neg)
    m1 = jnp.max(lg, axis=-1, keepdims=True)
    i1 = jnp.min(jnp.where(lg == m1, lanef, float(LANES)), axis=-1, keepdims=True)
    lg2 = jnp.where(lanef == i1, neg, lg)
    m2 = jnp.max(lg2, axis=-1, keepdims=True)
    i2 = jnp.min(jnp.where(lg2 == m2, lanef, float(LANES)), axis=-1, keepdims=True)
    e2 = jnp.exp(m2 - m1)
    gate1 = 1.0 / (1.0 + e2)
    gate2 = e2 / (1.0 + e2)
    oh1 = (lanef == i1).astype(F32)
    oh2 = (lanef == i2).astype(F32)
    both = oh1 + oh2
    rr = lax.broadcasted_iota(jnp.int32, (tm, tm), 0)
    cc = lax.broadcasted_iota(jnp.int32, (tm, tm), 1)
    before = _dot((rr > cc).astype(BF16), both.astype(BF16)) + run_ref[...]
    r1 = jnp.sum(before * oh1, axis=-1, keepdims=True)
    r2 = jnp.sum(before * oh2, axis=-1, keepdims=True)
    run = run_ref[...] + jnp.sum(both, axis=0, keepdims=True)
    run_ref[...] = run
    cnt_ref[...] = jnp.broadcast_to(run, cnt_ref.shape)
    info = jnp.where(lane == 0, i1, jnp.where(lane == 1, i2, jnp.where(lane == 2, r1, jnp.where(
        lane == 3, r2, jnp.where(lane == 4, gate1, jnp.where(lane == 5, gate2, 0.0))))))
    info_ref[0] = info


def _moe_route(x, gain, sc1p, sh, w_router):
    bn, l, d = x.shape
    tm = min(256, l)
    wr = jnp.pad(w_router, ((0, 0), (0, LANES - w_router.shape[1])))
    vec = pl.BlockSpec((1, 1, d), lambda b, i: (b, 0, 0))
    return pl.pallas_call(
        functools.partial(_route_kernel, tm=tm),
        grid=(bn, l // tm),
        in_specs=[pl.BlockSpec((1, tm, d), lambda b, i: (b, i, 0)),
                  pl.BlockSpec((1, d), lambda b, i: (0, 0)), vec, vec,
                  pl.BlockSpec((d, LANES), lambda b, i: (0, 0))],
        out_specs=[pl.BlockSpec((1, tm, d), lambda b, i: (b, i, 0)),
                   pl.BlockSpec((1, tm, LANES), lambda b, i: (b, i, 0)),
                   pl.BlockSpec((8, LANES), lambda b, i: (0, 0))],
        out_shape=[jax.ShapeDtypeStruct((bn, l, d), F32),
                   jax.ShapeDtypeStruct((bn, l, LANES), F32),
                   jax.ShapeDtypeStruct((8, LANES), F32)],
        scratch_shapes=[pltpu.VMEM((1, LANES), F32)],
        compiler_params=_cp(("arbitrary", "arbitrary")),
        name="moe_route",
    )(x, gain.reshape(1, d), sc1p, sh, wr)


def _row_copy(src, dst, s, t, sem):
    return pltpu.make_async_copy(src.at[pl.ds(s, 1)], dst.at[pl.ds(t, 1)], sem)


def _dispatch_kernel(dest_ref, tok_ref, xb_in_ref, xb_ref, sem, *, tb):
    del xb_in_ref

    def start(t, _):
        _row_copy(tok_ref, xb_ref, t, dest_ref[0, 0, t], sem).start()
        _row_copy(tok_ref, xb_ref, t, dest_ref[0, 1, t], sem).start()
        return 0

    lax.fori_loop(0, tb, start, 0)

    def wait(t, _):
        _row_copy(tok_ref, xb_ref, t, dest_ref[0, 0, t], sem).wait()
        _row_copy(tok_ref, xb_ref, t, dest_ref[0, 1, t], sem).wait()
        return 0

    lax.fori_loop(0, tb, wait, 0)


def _moe_dispatch(tok, dest, n_slots):
    t, d = tok.shape
    nb, _, tb = dest.shape
    xb0 = jnp.zeros((n_slots, d), F32)
    return pl.pallas_call(
        functools.partial(_dispatch_kernel, tb=tb),
        grid=(nb,),
        in_specs=[pl.BlockSpec((1, 2, tb), lambda i: (i, 0, 0), memory_space=pltpu.SMEM),
                  pl.BlockSpec((tb, d), lambda i: (i, 0)),
                  pl.BlockSpec(memory_space=pl.ANY)],
        out_specs=pl.BlockSpec(memory_space=pl.ANY),
        out_shape=jax.ShapeDtypeStruct((n_slots, d), F32),
        scratch_shapes=[pltpu.SemaphoreType.DMA(())],
        input_output_aliases={2: 0},
        compiler_params=_cp(("arbitrary",)),
        name="moe_dispatch",
    )(dest, tok, xb0)


def _exp_up_kernel(be_ref, x_ref, wg_ref, wu_ref, o_ref, xs_ref):
    del be_ref

    @pl.when(pl.program_id(1) == 0)
    def _():
        xs_ref[...] = x_ref[...].astype(BF16)

    x = xs_ref[...]
    o_ref[...] = (_silu(_dot(x, wg_ref[...])) * _dot(x, wu_ref[...])).astype(o_ref.dtype)


def _moe_up(xb, block_e, w_gate, w_up, blk):
    n_slots, d = xb.shape
    ff = w_gate.shape[2]
    tn = _col_tile(ff)
    wspec = pl.BlockSpec((None, d, tn), lambda i, j, be: (be[i], 0, j))
    grid_spec = pltpu.PrefetchScalarGridSpec(
        num_scalar_prefetch=1,
        grid=(n_slots // blk, ff // tn),
        in_specs=[pl.BlockSpec((blk, d), lambda i, j, be: (i, 0)), wspec, wspec],
        out_specs=pl.BlockSpec((blk, tn), lambda i, j, be: (i, j)),
        scratch_shapes=[pltpu.VMEM((blk, d), BF16)],
    )
    return pl.pallas_call(
        _exp_up_kernel,
        grid_spec=grid_spec,
        out_shape=jax.ShapeDtypeStruct((n_slots, ff), BF16),
        compiler_params=_cp(("parallel", "arbitrary")),
        name="moe_expert_up",
    )(block_e, xb, w_gate, w_up)


def _exp_down_kernel(be_ref, h_ref, w_ref, o_ref):
    del be_ref
    o_ref[...] = _dot(h_ref[...], w_ref[...])


def _moe_down(hmid, block_e, w_down, blk):
    n_slots, ff = hmid.shape
    d = w_down.shape[2]
    grid_spec = pltpu.PrefetchScalarGridSpec(
        num_scalar_prefetch=1,
        grid=(n_slots // blk,),
        in_specs=[pl.BlockSpec((blk, ff), lambda i, be: (i, 0)),
                  pl.BlockSpec((None, ff, d), lambda i, be: (be[i], 0, 0))],
        out_specs=pl.BlockSpec((blk, d), lambda i, be: (i, 0)),
    )
    return pl.pallas_call(
        _exp_down_kernel,
        grid_spec=grid_spec,
        out_shape=jax.ShapeDtypeStruct((n_slots, d), F32),
        compiler_params=_cp(("arbitrary",)),
        name="moe_expert_down",
    )(block_e, hmid, w_down)


def _combine_kernel(dest_ref, yb_ref, info_ref, x_ref, g_ref, o_ref, y_ref, sem, *, tb):
    def start(t, _):
        _row_copy(yb_ref, y_ref.at[0], dest_ref[0, 0, 0, t], t, sem).start()
        _row_copy(yb_ref, y_ref.at[1], dest_ref[0, 0, 1, t], t, sem).start()
        return 0

    lax.fori_loop(0, tb, start, 0)

    def wait(t, _):
        _row_copy(yb_ref, y_ref.at[0], dest_ref[0, 0, 0, t], t, sem).wait()
        _row_copy(yb_ref, y_ref.at[1], dest_ref[0, 0, 1, t], t, sem).wait()
        return 0

    lax.fori_loop(0, tb, wait, 0)
    info = info_ref[0]
    lane = lax.broadcasted_iota(jnp.int32, info.shape, 1)
    gate1 = jnp.sum(jnp.where(lane == 4, info, 0.0), axis=-1, keepdims=True)
    gate2 = jnp.sum(jnp.where(lane == 5, info, 0.0), axis=-1, keepdims=True)
    o_ref[0] = x_ref[0] + g_ref[0] * (gate1 * y_ref[0] + gate2 * y_ref[1])


def _moe_combine(yb, dest, info, x, gate):
    bn, l, d = x.shape
    tb = dest.shape[-1]
    nb = l // tb
    dest4 = dest.reshape(bn, nb, 2, tb)
    return pl.pallas_call(
        functools.partial(_combine_kernel, tb=tb),
        grid=(bn, nb),
        in_specs=[pl.BlockSpec((1, 1, 2, tb), lambda b, i: (b, i, 0, 0), memory_space=pltpu.SMEM),
                  pl.BlockSpec(memory_space=pl.ANY),
                  pl.BlockSpec((1, tb, LANES), lambda b, i: (b, i, 0)),
                  pl.BlockSpec((1, tb, d), lambda b, i: (b, i, 0)),
                  pl.BlockSpec((1, 1, d), lambda b, i: (b, 0, 0))],
        out_specs=pl.BlockSpec((1, tb, d), lambda b, i: (b, i, 0)),
        out_shape=jax.ShapeDtypeStruct(x.shape, F32),
        scratch_shapes=[pltpu.VMEM((2, tb, d), F32), pltpu.SemaphoreType.DMA(())],
        input_output_aliases={3: 0},
        compiler_params=_cp(("arbitrary", "arbitrary")),
        name="moe_combine",
    )(dest4, yb, info, x, gate)


def _moe_layer(x, gain, sc1p, sh, gate, w_router, w_gate, w_up, w_down, blk):
    bn, l, d = x.shape
    t = bn * l
    tok, info, cnt = _moe_route(x, gain, sc1p, sh, w_router)
    counts = cnt[0, :N_EXPERTS].astype(jnp.int32)
    padded = (counts + blk - 1) // blk * blk
    pad_end = jnp.cumsum(padded)
    pad_start = pad_end - padded
    n_blocks = -(-(2 * t) // blk) + N_EXPERTS
    block_e = jnp.minimum(
        jnp.sum((jnp.arange(n_blocks, dtype=jnp.int32)[:, None] * blk >= pad_end[None, :]).astype(jnp.int32), axis=1),
        N_EXPERTS - 1).astype(jnp.int32)
    info2 = info.reshape(t, LANES)
    ee = jnp.arange(N_EXPERTS, dtype=jnp.int32)[None, :]

    def slot(e_col, r_col):
        e = info2[:, e_col].astype(jnp.int32)
        return jnp.sum(jnp.where(e[:, None] == ee, pad_start[None, :], 0), axis=1) + info2[:, r_col].astype(jnp.int32)

    tb = min(256, l)
    dest = jnp.stack([slot(0, 2), slot(1, 3)], axis=0).reshape(2, t // tb, tb).transpose(1, 0, 2)
    xb = _moe_dispatch(tok.reshape(t, d), dest, n_blocks * blk)
    hmid = _moe_up(xb, block_e, w_gate, w_up, blk)
    yb = _moe_down(hmid, block_e, w_down, blk)
    return _moe_combine(yb, dest, info, x, gate)


def kernel(x, c, ctx, c_ctx, ada_w, ada_b, norm1_g, norm2_g, normf_g, cf_w_pw1, cf_b_pw1, cf_w_dw, cf_b_dw, cf_ln_g, cf_ln_b, cf_w_pw2, cf_b_pw2, hy_w_in, hy_b_in, hy_w_short, hy_b_short, hy_w_f1, hy_b_f1, hy_freq, hy_w_f2, hy_b_f2, hy_w_f3, hy_bias, hy_w_out, hy_b_out, hg_lb_logits, hg_w_in, hg_gn_g, hg_w_out, ffn_w_gate, ffn_w_up, ffn_w_down, moe_w_router, moe_w_gate, moe_w_up, moe_w_down):
    bn, s, d = x.shape
    cl = ctx.shape[1]
    depth = ada_w.shape[0]
    bf = lambda a: a.astype(BF16)

    cvecs = jnp.concatenate([c, c_ctx[None], jnp.zeros((8 - bn - 1, d), F32)], axis=0)
    mods = _ada_mod(cvecs, ada_w, ada_b)

    def mod_slices(i, ctx_rows):
        m = jnp.broadcast_to(mods[i, bn:bn + 1], (bn, N_MOD * d)) if ctx_rows else mods[i, :bn]
        parts = [m[:, k * d:(k + 1) * d].reshape(bn, 1, d) for k in range(N_MOD)]
        sh1, sc1, g1, sh2, sc2, g2 = parts
        return sh1, 1.0 + sc1, g1, sh2, 1.0 + sc2, g2

    p = jax.nn.softmax(hg_lb_logits.astype(F32), axis=0)
    lb_all = jnp.cumsum(p, axis=0) - p[0]

    x = _embed(x)
    xc = ctx

    for i in range(depth):
        last = i == depth - 1
        kind, j = i % 3, i // 3
        streams = [(x, mod_slices(i, False), False)]
        if not last:
            streams.append((xc, mod_slices(i, True), True))

        new = []
        if kind == 0:
            wa, wg = bf(cf_w_pw1[j][:, :d]), bf(cf_w_pw1[j][:, d:])
            ba, bg = cf_b_pw1[j][:d], cf_b_pw1[j][d:]
            w2 = bf(cf_w_pw2[j])
            for xs, (sh1, sc1, g1, _, _, _), _ in streams:
                u = _nmm(xs, norm1_g[i], sc1, sh1, [wa, wg], [ba, bg], epi="glu")
                new.append(_conformer_tail(u, cf_w_dw[j], cf_b_dw[j], cf_ln_g[j], cf_ln_b[j], w2, cf_b_pw2[j], xs, g1))
        elif kind == 1:
            order = hy_bias.shape[1]
            w_in_t = bf(hy_w_in[j].T)
            w_out = bf(hy_w_out[j])
            for xs, (sh1, sc1, g1, _, _, _), _ in streams:
                l = xs.shape[1]
                n2, r, n1, lp = _fft_plan(l)
                consts = _fft_consts(n1, n2)
                filt = _hyena_filters_t(l, lp, hy_w_f1[j], hy_b_f1[j], hy_freq[j], hy_w_f2[j], hy_b_f2[j],
                                        hy_w_f3[j], order, d)
                spec = _hyena_spectrum(filt, consts, n1, n2)
                ut = _nmm(xs, norm1_g[i], sc1, sh1, [w_in_t], [hy_b_in[j]], transposed=True)
                if lp > l:
                    ut = jnp.pad(ut, ((0, 0), (0, 0), (0, lp - l)))
                zt = _hyena_conv(ut, hy_w_short[j], hy_b_short[j], hy_bias[j], spec, consts, n1, n2, l)
                new.append(_mmres(zt, w_out, hy_b_out[j], xs, g1, trans_a=True))
        else:
            w_in = bf(hg_w_in[j])
            w_out = bf(hg_w_out[j])
            nh = d // HG_HEAD
            state = jnp.zeros((bn, 2, nh, HG_HEAD, HG_HEAD), F32)
            outs = {}
            for xs, (sh1, sc1, g1, _, _, _), is_ctx in ((xc, mod_slices(i, True), True), streams[0]):
                u5 = _nmm(xs, norm1_g[i], sc1, sh1, [w_in], [jnp.zeros((5 * d,), F32)])
                o_f, o_b, state = _hgrn_scan(u5, lb_all[i], state)
                if not (is_ctx and last):
                    outs[is_ctx] = _hgrn_out(o_f, o_b, u5, hg_gn_g[j], w_out, xs, g1)
            new = [outs[False]] + ([outs[True]] if not last else [])

        x = new[0]
        if not last:
            xc = new[1]
        streams = [(x, mod_slices(i, False), False)]
        if not last:
            streams.append((xc, mod_slices(i, True), True))

        new = []
        if i % 2 == 0:
            wg, wu, wd = bf(ffn_w_gate[i // 2]), bf(ffn_w_up[i // 2]), bf(ffn_w_down[i // 2])
            zb = jnp.zeros((wg.shape[1],), F32)
            for xs, (_, _, _, sh2, sc2, g2), _ in streams:
                hmid = _nmm(xs, norm2_g[i], sc2, sh2, [wg, wu], [zb, zb], epi="swiglu", out_dtype=BF16)
                new.append(_mmres(hmid, wd, jnp.zeros((d,), F32), xs, g2))
        else:
            wg, wu, wd = bf(moe_w_gate[i // 2]), bf(moe_w_up[i // 2]), bf(moe_w_down[i // 2])
            for xs, (_, _, _, sh2, sc2, g2), is_ctx in streams:
                new.append(_moe_layer(xs, norm2_g[i], sc2, sh2, g2, moe_w_router[i // 2], wg, wu, wd,
                                      blk=128 if is_ctx else 512))
        x = new[0]
        if not last:
            xc = new[1]

    return _final_norm(x, normf_g)


def _fnorm_kernel(x_ref, g_ref, o_ref):
    xv = x_ref[0]
    o_ref[0] = xv * lax.rsqrt(jnp.mean(xv * xv, axis=-1, keepdims=True) + EPS) * g_ref[...]


def _final_norm(x, g):
    bn, l, d = x.shape
    tm = min(512, l)
    return pl.pallas_call(
        _fnorm_kernel,
        grid=(bn, l // tm),
        in_specs=[pl.BlockSpec((1, tm, d), lambda b, i: (b, i, 0)), pl.BlockSpec((1, d), lambda b, i: (0, 0))],
        out_specs=pl.BlockSpec((1, tm, d), lambda b, i: (b, i, 0)),
        out_shape=jax.ShapeDtypeStruct(x.shape, F32),
        compiler_params=_cp(("parallel", "parallel")),
        name="final_norm",
    )(x, g.reshape(1, d))
```

```python
import functools
import math

import numpy as np
import jax
import jax.numpy as jnp
from jax import lax
from jax.experimental import pallas as pl
from jax.experimental.pallas import tpu as pltpu

F32 = jnp.float32
BF16 = jnp.bfloat16
EPS = 1e-6

GRID_W = 64
N_MOD = 6
CONV_WIDTH = 31
CONV_HALO = 16
HY_EMB = 33
HY_EMB_PAD = 40
HY_TARGET = 1e-2
HY_DECAY_MIN = math.log(HY_TARGET) / 1.5
HY_DECAY_MAX = math.log(HY_TARGET) / 0.3
HG_HEAD = 128
HG_CHUNK = 64
N_EXPERTS = 8
LANES = 128
DMA_UNROLL = 8
VMEM_LIMIT = 56 * 1024 * 1024


def _cp(sem, vmem=VMEM_LIMIT):
    return pltpu.CompilerParams(dimension_semantics=sem, vmem_limit_bytes=vmem)


def _dot(a, b):
    return jnp.dot(a, b, preferred_element_type=F32)


def _dot_nt(a, b):
    return lax.dot_general(a, b, (((1,), (1,)), ((), ())), preferred_element_type=F32)


def _dot_tn(a, b):
    return lax.dot_general(a, b, (((0,), (0,)), ((), ())), preferred_element_type=F32)


def _split3(a):
    h = a.astype(BF16)
    r = a - h.astype(F32)
    m = r.astype(BF16)
    l = (r - m.astype(F32)).astype(BF16)
    return h, m, l


def _dot_acc(a, b, dot=_dot):
    ah, am, _ = _split3(a)
    bh, bm, _ = _split3(b)
    return dot(ah, bh) + (dot(ah, bm) + dot(am, bh))


def _silu(x):
    return x * jax.nn.sigmoid(x)


def _ada_kernel(c_ref, w_ref, b_ref, o_ref):
    s = _silu(c_ref[...])
    o_ref[0] = _dot_acc(s, w_ref[0]) + b_ref[0]


def _ada_mod(cvecs, ada_w, ada_b):
    depth, d, n = ada_w.shape
    tn = 1024
    return pl.pallas_call(
        _ada_kernel,
        grid=(depth, n // tn),
        in_specs=[pl.BlockSpec((8, d), lambda i, j: (0, 0)),
                  pl.BlockSpec((1, d, tn), lambda i, j: (i, 0, j)),
                  pl.BlockSpec((1, 1, tn), lambda i, j: (i, 0, j))],
        out_specs=pl.BlockSpec((1, 8, tn), lambda i, j: (i, 0, j)),
        out_shape=jax.ShapeDtypeStruct((depth, 8, n), F32),
        compiler_params=_cp(("parallel", "parallel")),
        name="ada_mod",
    )(cvecs, ada_w, ada_b.reshape(depth, 1, n))


def _embed_kernel(x_ref, om_ref, o_ref, pos_ref, *, tm):
    i = pl.program_id(0)

    @pl.when(pl.program_id(1) == 0)
    def _():
        q = om_ref.shape[1]
        t = (lax.broadcasted_iota(jnp.int32, (tm, 1), 0) + i * tm).astype(F32)
        r = jnp.floor(t / GRID_W)
        col = t - r * GRID_W
        a = r * om_ref[...]
        b = col * om_ref[...]
        pos_ref[:, 0 * q:1 * q] = jnp.sin(a)
        pos_ref[:, 1 * q:2 * q] = jnp.cos(a)
        pos_ref[:, 2 * q:3 * q] = jnp.sin(b)
        pos_ref[:, 3 * q:4 * q] = jnp.cos(b)

    o_ref[0] = x_ref[0] + pos_ref[...]


def _embed(x):
    bn, s, d = x.shape
    tm = min(512, s)
    q = d // 4
    omega = (1.0 / (10000.0 ** (jnp.arange(q, dtype=F32) / q))).reshape(1, q)
    return pl.pallas_call(
        functools.partial(_embed_kernel, tm=tm),
        grid=(s // tm, bn),
        in_specs=[pl.BlockSpec((1, tm, d), lambda i, b: (b, i, 0)),
                  pl.BlockSpec((1, q), lambda i, b: (0, 0))],
        out_specs=pl.BlockSpec((1, tm, d), lambda i, b: (b, i, 0)),
        out_shape=jax.ShapeDtypeStruct(x.shape, F32),
        scratch_shapes=[pltpu.VMEM((tm, d), F32)],
        compiler_params=_cp(("parallel", "arbitrary")),
        name="pos_embed",
    )(x, omega)


def _norm_mod(x, g, sc1p, sh):
    ms = jnp.mean(x * x, axis=-1, keepdims=True)
    return (x * lax.rsqrt(ms + EPS) * g) * sc1p + sh


def _nmm_kernel(*refs, nw, epi, transposed):
    x_ref, g_ref, sc_ref, sh_ref = refs[:4]
    w_refs = refs[4:4 + nw]
    b_refs = refs[4 + nw:4 + 2 * nw]
    o_ref = refs[4 + 2 * nw]
    hs_ref = refs[5 + 2 * nw]

    @pl.when(pl.program_id(2) == 0)
    def _():
        hs_ref[...] = _norm_mod(x_ref[0], g_ref[...], sc_ref[0], sh_ref[0]).astype(BF16)

    h = hs_ref[...]
    if transposed:
        r = _dot_nt(w_refs[0][...], h) + b_refs[0][...]
    else:
        outs = [_dot(h, w[...]) + b[...] for w, b in zip(w_refs, b_refs)]
        if epi == "plain":
            r = outs[0]
        elif epi == "glu":
            r = outs[0] * jax.nn.sigmoid(outs[1])
        else:
            r = _silu(outs[0]) * outs[1]
    o_ref[0] = r.astype(o_ref.dtype)


def _col_tile(n, target=1024):
    best = None
    for t in range(256, n + 1, 256):
        if n % t == 0 and t <= target * 7 // 4:
            best = t
    return best if best is not None else n


def _nmm(x, gain, sc1p, sh, ws, bs, *, epi="plain", out_dtype=F32, transposed=False):
    bn, l, d = x.shape
    n = ws[0].shape[0] if transposed else ws[0].shape[1]
    tn = _col_tile(n)
    tm = min(1024 if len(ws) * tn <= 2048 else 512, l)
    nw = len(ws)
    if transposed:
        w_specs = [pl.BlockSpec((tn, d), lambda b, i, j: (j, 0))]
        b_specs = [pl.BlockSpec((tn, 1), lambda b, i, j: (j, 0))]
        bs = [bs[0].reshape(n, 1)]
        out_spec = pl.BlockSpec((1, tn, tm), lambda b, i, j: (b, j, i))
        out_shape = jax.ShapeDtypeStruct((bn, n, l), out_dtype)
    else:
        w_specs = [pl.BlockSpec((d, tn), lambda b, i, j: (0, j)) for _ in ws]
        b_specs = [pl.BlockSpec((1, tn), lambda b, i, j: (0, j)) for _ in ws]
        bs = [b.reshape(1, n) for b in bs]
        out_spec = pl.BlockSpec((1, tm, tn), lambda b, i, j: (b, i, j))
        out_shape = jax.ShapeDtypeStruct((bn, l, n), out_dtype)
    vec = pl.BlockSpec((1, 1, d), lambda b, i, j: (b, 0, 0))
    return pl.pallas_call(
        functools.partial(_nmm_kernel, nw=nw, epi=epi, transposed=transposed),
        grid=(bn, l // tm, n // tn),
        in_specs=[pl.BlockSpec((1, tm, d), lambda b, i, j: (b, i, 0)),
                  pl.BlockSpec((1, d), lambda b, i, j: (0, 0)), vec, vec] + w_specs + b_specs,
        out_specs=out_spec,
        out_shape=out_shape,
        scratch_shapes=[pltpu.VMEM((tm, d), BF16)],
        compiler_params=_cp(("parallel", "parallel", "arbitrary")),
        name="norm_mod_matmul_" + ("t" if transposed else epi),
    )(x, gain.reshape(1, d), sc1p, sh, *ws, *bs)


def _mmres_kernel(a_ref, w_ref, b_ref, x_ref, g_ref, o_ref, *, trans_a):
    a = a_ref[0]
    y = (_dot_tn(a, w_ref[...]) if trans_a else _dot(a, w_ref[...])) + b_ref[...]
    o_ref[0] = x_ref[0] + g_ref[0] * y


def _mmres(a, w, bias, x, gate, *, trans_a=False):
    bn, l, n = x.shape
    k = w.shape[0]
    tm = min(512, l)
    a_spec = (pl.BlockSpec((1, k, tm), lambda b, i: (b, 0, i)) if trans_a
              else pl.BlockSpec((1, tm, k), lambda b, i: (b, i, 0)))
    return pl.pallas_call(
        functools.partial(_mmres_kernel, trans_a=trans_a),
        grid=(bn, l // tm),
        in_specs=[a_spec,
                  pl.BlockSpec((k, n), lambda b, i: (0, 0)),
                  pl.BlockSpec((1, n), lambda b, i: (0, 0)),
                  pl.BlockSpec((1, tm, n), lambda b, i: (b, i, 0)),
                  pl.BlockSpec((1, 1, n), lambda b, i: (b, 0, 0))],
        out_specs=pl.BlockSpec((1, tm, n), lambda b, i: (b, i, 0)),
        out_shape=jax.ShapeDtypeStruct(x.shape, F32),
        input_output_aliases={3: 0},
        compiler_params=_cp(("parallel", "parallel")),
        name="matmul_residual" + ("_ta" if trans_a else ""),
    )(a, w, bias.reshape(1, n), x, gate)


def _cf2_kernel(prev_ref, cur_ref, next_ref, wdw_ref, bdw_ref, lng_ref, lnb_ref, w2_ref, b2_ref,
                x_ref, g_ref, o_ref, buf_ref, sh_ref, cv_ref, *, tm):
    i = pl.program_id(1)
    nt = pl.num_programs(1)
    h = CONV_HALO
    buf_ref[0:h] = jnp.where(i > 0, prev_ref[0], 0.0)
    buf_ref[h:h + tm] = cur_ref[0]
    buf_ref[h + tm:2 * h + tm] = jnp.where(i < nt - 1, next_ref[0], 0.0)
    span = sh_ref.shape[1]
    for s in range(8):
        sh_ref[s] = buf_ref[s:s + span]
    off = h - (CONV_WIDTH - 1) // 2
    rc = 8
    for c in range(tm // rc):
        acc = None
        for k in range(CONV_WIDTH):
            q, s = divmod(off + k, 8)
            term = sh_ref[s, c * rc + 8 * q:c * rc + 8 * q + rc] * wdw_ref[k]
            acc = term if acc is None else acc + term
        cv_ref[c * rc:(c + 1) * rc] = acc + bdw_ref[...]
    u = cv_ref[...]
    mu = jnp.mean(u, axis=-1, keepdims=True)
    uc = u - mu
    var = jnp.mean(uc * uc, axis=-1, keepdims=True)
    v = _silu(uc * lax.rsqrt(var + EPS) * lng_ref[...] + lnb_ref[...])
    y = _dot(v.astype(BF16), w2_ref[...]) + b2_ref[...]
    o_ref[0] = x_ref[0] + g_ref[0] * y


def _conformer_tail(u, w_dw, b_dw, ln_g, ln_b, w2, b2, x, gate):
    bn, l, d = x.shape
    tm = min(256, l)
    h = CONV_HALO
    nh = l // h
    r = tm // h
    row = lambda a: a.reshape(1, d)
    full = lambda shp: pl.BlockSpec(shp, lambda b, i: (0,) * len(shp))
    return pl.pallas_call(
        functools.partial(_cf2_kernel, tm=tm),
        grid=(bn, l // tm),
        in_specs=[pl.BlockSpec((1, h, d), lambda b, i: (b, jnp.maximum(i * r - 1, 0), 0)),
                  pl.BlockSpec((1, tm, d), lambda b, i: (b, i, 0)),
                  pl.BlockSpec((1, h, d), lambda b, i: (b, jnp.minimum((i + 1) * r, nh - 1), 0)),
                  full((CONV_WIDTH, 8, d)), full((1, d)), full((1, d)), full((1, d)),
                  full((d, d)), full((1, d)),
                  pl.BlockSpec((1, tm, d), lambda b, i: (b, i, 0)),
                  pl.BlockSpec((1, 1, d), lambda b, i: (b, 0, 0))],
        out_specs=pl.BlockSpec((1, tm, d), lambda b, i: (b, i, 0)),
        out_shape=jax.ShapeDtypeStruct(x.shape, F32),
        scratch_shapes=[pltpu.VMEM((tm + 2 * h, d), F32),
                        pltpu.VMEM((8, tm + 2 * h - 8, d), F32),
                        pltpu.VMEM((tm, d), F32)],
        input_output_aliases={9: 0},
        compiler_params=_cp(("parallel", "parallel")),
        name="conformer_tail",
    )(u, u, u, jnp.broadcast_to(w_dw[:, None, :], (CONV_WIDTH, 8, d)), row(b_dw), row(ln_g), row(ln_b),
      w2, row(b2), x, gate)


def _fft_plan(l):
    n2 = 256 if l >= 4096 else 128
    r = max(16, -(-l // n2))
    return n2, r, 2 * r, r * n2


def _fft_channels(r, n2, d):
    return max(8, min(64, d, (128 * 1024) // (r * n2)))


def _fft_consts(n1, n2):
    r = n1 // 2
    n = n1 * n2
    k1 = np.arange(n1, dtype=np.float64)[:, None]
    a1 = 2.0 * np.pi * k1 * np.arange(r, dtype=np.float64)[None, :] / n1
    fc1 = np.concatenate([np.cos(a1), -np.sin(a1)], axis=0)
    at = 2.0 * np.pi * k1 * np.arange(n2, dtype=np.float64)[None, :] / n
    tw = np.stack([np.cos(at), -np.sin(at)], axis=0)
    a2 = 2.0 * np.pi * np.outer(np.arange(n2, dtype=np.float64), np.arange(n2, dtype=np.float64)) / n2
    f2 = np.concatenate([np.cos(a2), -np.sin(a2)], axis=1)
    g2 = np.concatenate([np.cos(a2), np.sin(a2)], axis=1)
    ai = 2.0 * np.pi * np.arange(r, dtype=np.float64)[:, None] * np.arange(n1, dtype=np.float64)[None, :] / n1
    c1i = np.concatenate([np.cos(ai), -np.sin(ai)], axis=1) / n
    return (jnp.asarray(fc1, BF16), jnp.asarray(tw, F32), jnp.asarray(f2, BF16),
            jnp.asarray(g2, BF16), jnp.asarray(c1i, BF16))


def _fft_fwd_rows(z, fc1, tr, ti):
    n1 = tr.shape[0]
    a = _dot(fc1, z.astype(BF16))
    ar, ai = a[:n1], a[n1:]
    return ar * tr - ai * ti, ar * ti + ai * tr


def _cplx_rows(p, ch, n1, n2):
    p4 = p.reshape(ch, 2, n1, 2 * n2)
    return p4[:, 0, :, :n2] - p4[:, 1, :, n2:], p4[:, 0, :, n2:] + p4[:, 1, :, :n2]


def _hyfilt_kernel(w1_ref, b1_ref, fr_ref, w2_ref, b2_ref, fb_ref, w3_ref, dl_ref, o_ref, hid_ref, *, l, lp):
    lane = lax.broadcasted_iota(jnp.int32, (1, lp), 1)
    pos = lane.astype(F32)
    t = pos / (l - 1)

    @pl.when((pl.program_id(0) == 0) & (pl.program_id(1) == 0))
    def _():
        w = (2.0 * math.pi) * pos / l
        arg = fb_ref[...] * w
        row = lax.broadcasted_iota(jnp.int32, (HY_EMB_PAD, 1), 0)
        bands = (HY_EMB - 1) // 2
        z = jnp.where(row == 0, t,
                      jnp.where(row <= bands, jnp.cos(arg),
                                jnp.where(row < HY_EMB, -jnp.sin(arg), 0.0)))
        fr = fr_ref[...]
        hid = jnp.sin(fr * (_dot_acc(w1_ref[...], z) + b1_ref[...]))
        for j in range(w2_ref.shape[0]):
            hid = jnp.sin(fr * (_dot_acc(w2_ref[j], hid) + b2_ref[j]))
        hid_ref[...] = hid

    decay = jnp.exp(-t * dl_ref[...])
    hid = hid_ref[...]
    valid = lane < l
    f0 = jnp.where(valid, _dot_acc(w3_ref[0, 0], hid) * decay, 0.0)
    f1 = jnp.where(valid & (lane > 0), _dot_acc(w3_ref[0, 1], hid) * decay, 0.0)
    norm = jnp.sum(jnp.abs(f0), axis=-1, keepdims=True) + jnp.sum(jnp.abs(f1), axis=-1, keepdims=True)
    o_ref[0, 0] = f0 / norm
    o_ref[0, 1] = f1 / norm


def _hyena_filters_t(l, lp, w_f1, b_f1, freq, w_f2, b_f2, w_f3, order, d):
    fw = w_f1.shape[1]
    bands = (HY_EMB - 1) // 2
    fb = jnp.linspace(1e-4, bands - 1, bands, dtype=F32)
    fbcol = jnp.concatenate([jnp.zeros((1,), F32), fb, fb, jnp.zeros((HY_EMB_PAD - HY_EMB,), F32)]).reshape(-1, 1)
    w1t = jnp.pad(w_f1.T, ((0, 0), (0, HY_EMB_PAD - HY_EMB)))
    w2t = jnp.swapaxes(w_f2, 1, 2)
    w3t = w_f3.T.reshape(order, 2, d, fw)
    deltas = jnp.abs(jnp.linspace(HY_DECAY_MIN, HY_DECAY_MAX, d, dtype=F32)).reshape(d, 1)
    tr = 32
    c2 = lambda shp: pl.BlockSpec(shp, lambda o, c: (0,) * len(shp))
    return pl.pallas_call(
        functools.partial(_hyfilt_kernel, l=l, lp=lp),
        grid=(order, d // tr),
        in_specs=[c2((fw, HY_EMB_PAD)), c2((fw, 1)), c2((fw, 1)), c2(w2t.shape), c2((w_f2.shape[0], fw, 1)),
                  c2((HY_EMB_PAD, 1)),
                  pl.BlockSpec((1, 2, tr, fw), lambda o, c: (o, 0, c, 0)),
                  pl.BlockSpec((tr, 1), lambda o, c: (c, 0))],
        out_specs=pl.BlockSpec((1, 2, tr, lp), lambda o, c: (o, 0, c, 0)),
        out_shape=jax.ShapeDtypeStruct((order, 2, d, lp), F32),
        scratch_shapes=[pltpu.VMEM((fw, lp), F32)],
        compiler_params=_cp(("arbitrary", "arbitrary")),
        name="hyena_filter",
    )(w1t, b_f1.reshape(fw, 1), freq.reshape(fw, 1), w2t, b_f2.reshape(-1, fw, 1), fbcol, w3t, deltas)


def _hyspec_kernel(f_ref, fc1_ref, tw_ref, f2_ref, o_ref, ys_ref, *, dc, n1, n2):
    tr, ti = tw_ref[0], tw_ref[1]
    fc1 = fc1_ref[...]

    def stage1(c, _):
        for s in range(2):
            yr, yi = _fft_fwd_rows(f_ref[0, s, c], fc1, tr, ti)
            ys_ref[s, c, :n1] = yr.astype(BF16)
            ys_ref[s, c, n1:] = yi.astype(BF16)
        return 0

    for c in range(dc):
        stage1(c, 0)
    zs = []
    for s in range(2):
        p = _dot(ys_ref[s].reshape(dc * 2 * n1, n2), f2_ref[...])
        zs.append(_cplx_rows(p, dc, n1, n2))
    o_ref[0, 0] = zs[0][0] + zs[1][0]
    o_ref[0, 1] = zs[0][1] - zs[1][1]


def _hyena_spectrum(filt, consts, n1, n2):
    order, _, d, lp = filt.shape
    r = n1 // 2
    dc = _fft_channels(r, n2, d)
    fc1, tw, f2, _, _ = consts
    c2 = lambda a: pl.BlockSpec(a.shape, lambda o, c: (0,) * a.ndim)
    return pl.pallas_call(
        functools.partial(_hyspec_kernel, dc=dc, n1=n1, n2=n2),
        grid=(order, d // dc),
        in_specs=[pl.BlockSpec((1, 2, dc, r, n2), lambda o, c: (o, 0, c, 0, 0)), c2(fc1), c2(tw), c2(f2)],
        out_specs=pl.BlockSpec((1, 2, dc, n1, n2), lambda o, c: (o, 0, c, 0, 0)),
        out_shape=jax.ShapeDtypeStruct((order, 2, d, n1, n2), F32),
        scratch_shapes=[pltpu.VMEM((2, dc, 2 * n1, n2), BF16)],
        compiler_params=_cp(("parallel", "parallel")),
        name="hyena_spectrum",
    )(filt.reshape(order, 2, d, r, n2), fc1, tw, f2)


def _hyconv_kernel(wsh_ref, hb_ref, v_ref, x1_ref, x2_ref, h_ref, fc1_ref, tw_ref, f2_ref, g2_ref, c1i_ref,
                   o_ref, cs_ref, ys_ref, vs_ref, *, dc, ch, r, n1, n2, d, l, order):
    cb = pl.program_id(0)
    tr, ti = tw_ref[0], tw_ref[1]
    row = lax.broadcasted_iota(jnp.int32, (r, n2), 0)
    lane = lax.broadcasted_iota(jnp.int32, (r, n2), 1)
    valid = (row * n2 + lane) < l

    def short_conv(c, _):
        chan = cb * dc + c
        for s, ref in enumerate((v_ref, x1_ref, x2_ref)):
            u = ref[0, 0, c]
            idx = s * d + chan
            a = pltpu.roll(u, 1, axis=1)
            prev = jnp.where(lane == 0, jnp.where(row == 0, 0.0, pltpu.roll(a, 1, axis=0)), a)
            b = pltpu.roll(u, n2 - 1, axis=1)
            nxt = jnp.where(lane == n2 - 1, jnp.where(row == r - 1, 0.0, pltpu.roll(b, r - 1, axis=0)), b)
            y = wsh_ref[0, idx] * prev + wsh_ref[1, idx] * u + wsh_ref[2, idx] * nxt + wsh_ref[3, idx]
            cs_ref[s, c] = jnp.where(valid, y, 0.0)
        return 0

    for c in range(dc):
        short_conv(c, 0)

    for n in range(order):
        def stage1(c, _):
            yr, yi = _fft_fwd_rows(cs_ref[0, c], fc1_ref[...], tr, ti)
            ys_ref[c, :n1] = yr.astype(BF16)
            ys_ref[c, n1:] = yi.astype(BF16)
            return 0

        for c in range(dc):
            stage1(c, 0)

        for c0 in range(0, dc, ch):
            p = _dot(ys_ref[c0:c0 + ch].reshape(ch * 2 * n1, n2), f2_ref[...])
            zr, zi = _cplx_rows(p, ch, n1, n2)
            hr, hi = h_ref[n, 0, c0:c0 + ch], h_ref[n, 1, c0:c0 + ch]
            vs_ref[c0:c0 + ch, :n1] = (zr * hr - zi * hi).astype(BF16)
            vs_ref[c0:c0 + ch, n1:] = (zr * hi + zi * hr).astype(BF16)
            q = _dot(vs_ref[c0:c0 + ch].reshape(ch * 2 * n1, n2), g2_ref[...])
            ur, ui = _cplx_rows(q, ch, n1, n2)
            ys_ref[c0:c0 + ch, :n1] = (ur * tr + ui * ti).astype(BF16)
            ys_ref[c0:c0 + ch, n1:] = (ui * tr - ur * ti).astype(BF16)

        def stage4(c, _):
            chan = cb * dc + c
            conv = _dot(c1i_ref[...], ys_ref[c])
            z = cs_ref[n + 1, c] * (conv + hb_ref[n, chan] * cs_ref[0, c])
            if n == order - 1:
                o_ref[0, c] = z.astype(o_ref.dtype)
            else:
                cs_ref[0, c] = z
            return 0

        for c in range(dc):
            stage4(c, 0)


def _hyena_conv(ut, w_short, b_short, hy_bias, spec, consts, n1, n2, l):
    bn, d3, lp = ut.shape
    d = d3 // 3
    order = hy_bias.shape[0]
    r = n1 // 2
    dc = _fft_channels(r, n2, d)
    ch = max(1, min(dc, 1024 // (2 * n1)))
    fc1, tw, f2, g2, c1i = consts
    wsh = jnp.concatenate([w_short, b_short.reshape(1, d3)], axis=0)
    u5 = ut.reshape(bn, 3, d, r, n2)
    cst = lambda a: pl.BlockSpec(a.shape, lambda c, b, *_: (0,) * a.ndim)
    stream = lambda s: pl.BlockSpec((1, 1, dc, r, n2), lambda c, b, *_: (b, s, c, 0, 0))
    grid_spec = pltpu.PrefetchScalarGridSpec(
        num_scalar_prefetch=2,
        grid=(d // dc, bn),
        in_specs=[stream(0), stream(1), stream(2),
                  pl.BlockSpec((order, 2, dc, n1, n2), lambda c, b, *_: (0, 0, c, 0, 0)),
                  cst(fc1), cst(tw), cst(f2), cst(g2), cst(c1i)],
        out_specs=pl.BlockSpec((1, dc, r, n2), lambda c, b, *_: (b, c, 0, 0)),
        scratch_shapes=[pltpu.VMEM((3, dc, r, n2), F32),
                        pltpu.VMEM((dc, 2 * n1, n2), BF16),
                        pltpu.VMEM((dc, 2 * n1, n2), BF16)],
    )
    out = pl.pallas_call(
        functools.partial(_hyconv_kernel, dc=dc, ch=ch, r=r, n1=n1, n2=n2, d=d, l=l, order=order),
        grid_spec=grid_spec,
        out_shape=jax.ShapeDtypeStruct((bn, d, r, n2), BF16),
        compiler_params=_cp(("parallel", "arbitrary")),
        name="hyena_conv",
    )(wsh, hy_bias, u5, u5, u5, spec, fc1, tw, f2, g2, c1i)
    return out.reshape(bn, d, lp)


def _scan_kernel(qf_ref, vf_ref, zf_ref, qb_ref, vb_ref, zb_ref, lb_ref, tri_ref, s0_ref,
                 of_ref, ob_ref, sf_ref, st_ref, *, tt, hp):
    i = pl.program_id(2)
    ck = HG_CHUNK
    nck = tt // ck

    @pl.when(i == 0)
    def _():
        st_ref[...] = s0_ref[0]

    for h in range(hp):
        hs = slice(h * HG_HEAD, (h + 1) * HG_HEAD)
        lb = lb_ref[:, hs]
        for di, (q_ref, v_ref, z_ref, o_ref) in enumerate(((qf_ref, vf_ref, zf_ref, of_ref),
                                                           (qb_ref, vb_ref, zb_ref, ob_ref))):
            tri = tri_ref[di]
            f = lb + (1.0 - lb) * jax.nn.sigmoid(z_ref[0, :, hs])
            g = jnp.log(f)
            k = 1.0 - f
            gh, gm, _ = _split3(g)
            b = _dot(tri, gh) + _dot(tri, gm)
            tot = [b[c * ck + ck - 1:c * ck + ck] if di == 0 else b[c * ck:c * ck + 1] for c in range(nck)]
            btot = jnp.concatenate([jnp.broadcast_to(r, (ck, HG_HEAD)) for r in tot], axis=0)
            qd = (_silu(q_ref[0, :, hs]) * jnp.exp(b)).astype(BF16)
            ki = (k * jnp.exp(-b)).astype(BF16)
            ke = (k * jnp.exp(btot - b)).astype(BF16)
            vb = v_ref[0, :, hs].astype(BF16)
            att = jnp.where(tri != 0, _dot_nt(qd, ki), 0.0)
            o_intra = _dot(att.astype(BF16), vb)
            kv = [_dot_tn(vb[c * ck:(c + 1) * ck], ke[c * ck:(c + 1) * ck]) for c in range(nck)]
            st = st_ref[di, h]
            inter = [None] * nck
            for c in (range(nck) if di == 0 else reversed(range(nck))):
                inter[c] = _dot_nt(qd[c * ck:(c + 1) * ck], st.astype(BF16))
                st = st * jnp.exp(tot[c]) + kv[c]
            st_ref[di, h] = st
            o_ref[0, :, hs] = o_intra + jnp.concatenate(inter, axis=0)

    @pl.when(i == pl.num_programs(2) - 1)
    def _():
        sf_ref[0] = st_ref[...]


def _hgrn_scan(u5, lb, s0):
    bn, l, d5 = u5.shape
    d = d5 // 5
    nh = d // HG_HEAD
    hp = 4 if nh % 4 == 0 else 1
    hw = hp * HG_HEAD
    ng = nh // hp
    tt = min(256, l)
    nt = l // tt
    col = lambda s, rev: pl.BlockSpec(
        (1, tt, hw), (lambda b, h, i: (b, nt - 1 - i, s * ng + h)) if rev else (lambda b, h, i: (b, i, s * ng + h)))
    st_spec = pl.BlockSpec((1, 2, hp, HG_HEAD, HG_HEAD), lambda b, h, i: (b, 0, h, 0, 0))
    t_idx = np.arange(tt)
    same = (t_idx[:, None] // HG_CHUNK) == (t_idx[None, :] // HG_CHUNK)
    tri = jnp.asarray(np.stack([same & (t_idx[:, None] >= t_idx[None, :]),
                                same & (t_idx[:, None] <= t_idx[None, :])]), BF16)
    return pl.pallas_call(
        functools.partial(_scan_kernel, tt=tt, hp=hp),
        grid=(bn, ng, nt),
        in_specs=[col(0, False), col(1, False), col(3, False), col(0, True), col(1, True), col(4, True),
                  pl.BlockSpec((1, hw), lambda b, h, i: (0, h)),
                  pl.BlockSpec((2, tt, tt), lambda b, h, i: (0, 0, 0)), st_spec],
        out_specs=[col(0, False), col(0, True), st_spec],
        out_shape=[jax.ShapeDtypeStruct((bn, l, d), F32), jax.ShapeDtypeStruct((bn, l, d), F32),
                   jax.ShapeDtypeStruct(s0.shape, F32)],
        scratch_shapes=[pltpu.VMEM((2, hp, HG_HEAD, HG_HEAD), F32)],
        compiler_params=_cp(("parallel", "parallel", "arbitrary")),
        name="hgrn2_scan",
    )(u5, u5, u5, u5, u5, u5, lb.reshape(1, d), tri, s0)


def _hgout_kernel(of_ref, ob_ref, og_ref, gn_ref, w_ref, x_ref, g_ref, o_ref, a_ref, *, nh):
    for h in range(nh):
        sl = slice(h * HG_HEAD, (h + 1) * HG_HEAD)
        o = of_ref[0, :, sl] + ob_ref[0, :, sl]
        o = o * lax.rsqrt(jnp.mean(o * o, axis=-1, keepdims=True) + EPS) * gn_ref[:, sl]
        a_ref[:, sl] = (o * _silu(og_ref[0, :, sl])).astype(BF16)
    o_ref[0] = x_ref[0] + g_ref[0] * _dot(a_ref[...], w_ref[...])


def _hgrn_out(o_f, o_b, u5, gn_g, w_out, x, gate):
    bn, l, d = x.shape
    nh = d // HG_HEAD
    tm = min(512, l)
    blk = lambda c: pl.BlockSpec((1, tm, d), lambda b, i: (b, i, c))
    return pl.pallas_call(
        functools.partial(_hgout_kernel, nh=nh),
        grid=(bn, l // tm),
        in_specs=[blk(0), blk(0), blk(2),
                  pl.BlockSpec((1, d), lambda b, i: (0, 0)),
                  pl.BlockSpec((d, d), lambda b, i: (0, 0)),
                  blk(0), pl.BlockSpec((1, 1, d), lambda b, i: (b, 0, 0))],
        out_specs=blk(0),
        out_shape=jax.ShapeDtypeStruct(x.shape, F32),
        scratch_shapes=[pltpu.VMEM((tm, d), BF16)],
        input_output_aliases={5: 0},
        compiler_params=_cp(("parallel", "parallel")),
        name="hgrn2_out",
    )(o_f, o_b, u5, gn_g.reshape(1, d), w_out, x, gate)


def _route_kernel(x_ref, g_ref, sc_ref, sh_ref, wr_ref, tok_ref, info_ref, cnt_ref, run_ref, *, tm):
    first = (pl.program_id(0) == 0) & (pl.program_id(1) == 0)

    @pl.when(first)
    def _():
        run_ref[...] = jnp.zeros_like(run_ref)

    h = _norm_mod(x_ref[0], g_ref[...], sc_ref[0], sh_ref[0])
    tok_ref[0] = h
    logits = _dot_acc(h, wr_ref[...])
    lane = lax.broadcasted_iota(jnp.int32, (tm, LANES), 1)
    lanef = lane.astype(F32)
    neg = jnp.float32(-jnp.inf)
    lg = jnp.where(lane < N_EXPERTS, logits, neg)
    m1 = jnp.max(lg, axis=-1, keepdims=True)
    i1 = jnp.min(jnp.where(lg == m1, lanef, float(LANES)), axis=-1, keepdims=True)
    lg2 = jnp.where(lanef == i1, neg, lg)
    m2 = jnp.max(lg2, axis=-1, keepdims=True)
    i2 = jnp.min(jnp.where(lg2 == m2, lanef, float(LANES)), axis=-1, keepdims=True)
    e2 = jnp.exp(m2 - m1)
    gate1 = 1.0 / (1.0 + e2)
    gate2 = e2 / (1.0 + e2)
    oh1 = (lanef == i1).astype(F32)
    oh2 = (lanef == i2).astype(F32)
    both = oh1 + oh2
    rr = lax.broadcasted_iota(jnp.int32, (tm, tm), 0)
    cc = lax.broadcasted_iota(jnp.int32, (tm, tm), 1)
    before = _dot((rr > cc).astype(BF16), both.astype(BF16)) + run_ref[...]
    r1 = jnp.sum(before * oh1, axis=-1, keepdims=True)
    r2 = jnp.sum(before * oh2, axis=-1, keepdims=True)
    run = run_ref[...] + jnp.sum(both, axis=0, keepdims=True)
    run_ref[...] = run
    cnt_ref[...] = jnp.broadcast_to(run, cnt_ref.shape)
    info = jnp.where(lane == 0, i1, jnp.where(lane == 1, i2, jnp.where(lane == 2, r1, jnp.where(
        lane == 3, r2, jnp.where(lane == 4, gate1, jnp.where(lane == 5, gate2, 0.0))))))
    info_ref[0] = info


def _moe_route(x, gain, sc1p, sh, w_router):
    bn, l, d = x.shape
    tm = min(256, l)
    wr = jnp.pad(w_router, ((0, 0), (0, LANES - w_router.shape[1])))
    vec = pl.BlockSpec((1, 1, d), lambda b, i: (b, 0, 0))
    return pl.pallas_call(
        functools.partial(_route_kernel, tm=tm),
        grid=(bn, l // tm),
        in_specs=[pl.BlockSpec((1, tm, d), lambda b, i: (b, i, 0)),
                  pl.BlockSpec((1, d), lambda b, i: (0, 0)), vec, vec,
                  pl.BlockSpec((d, LANES), lambda b, i: (0, 0))],
        out_specs=[pl.BlockSpec((1, tm, d), lambda b, i: (b, i, 0)),
                   pl.BlockSpec((1, tm, LANES), lambda b, i: (b, i, 0)),
                   pl.BlockSpec((8, LANES), lambda b, i: (0, 0))],
        out_shape=[jax.ShapeDtypeStruct((bn, l, d), F32),
                   jax.ShapeDtypeStruct((bn, l, LANES), F32),
                   jax.ShapeDtypeStruct((8, LANES), F32)],
        scratch_shapes=[pltpu.VMEM((1, LANES), F32)],
        compiler_params=_cp(("arbitrary", "arbitrary")),
        name="moe_route",
    )(x, gain.reshape(1, d), sc1p, sh, wr)


def _row_copy(src, dst, s, t, sem):
    return pltpu.make_async_copy(src.at[pl.ds(s, 1)], dst.at[pl.ds(t, 1)], sem)


def _dispatch_kernel(dest_ref, tok_ref, xb_in_ref, xb_ref, sem, *, tb):
    del xb_in_ref

    def start(g, _):
        for u in range(DMA_UNROLL):
            t = g * DMA_UNROLL + u
            _row_copy(tok_ref, xb_ref, t, dest_ref[0, 0, t], sem).start(priority=0)
            _row_copy(tok_ref, xb_ref, t, dest_ref[0, 1, t], sem).start(priority=1)
        return 0

    lax.fori_loop(0, tb // DMA_UNROLL, start, 0)
    for _ in range(2):
        pltpu.make_async_copy(tok_ref, xb_ref.at[pl.ds(0, tb)], sem).wait()


def _moe_dispatch(tok, dest, n_slots):
    t, d = tok.shape
    nb, _, tb = dest.shape
    xb0 = jnp.zeros((n_slots, d), F32)
    return pl.pallas_call(
        functools.partial(_dispatch_kernel, tb=tb),
        grid=(nb,),
        in_specs=[pl.BlockSpec((1, 2, tb), lambda i: (i, 0, 0), memory_space=pltpu.SMEM),
                  pl.BlockSpec((tb, d), lambda i: (i, 0)),
                  pl.BlockSpec(memory_space=pl.ANY)],
        out_specs=pl.BlockSpec(memory_space=pl.ANY),
        out_shape=jax.ShapeDtypeStruct((n_slots, d), F32),
        scratch_shapes=[pltpu.SemaphoreType.DMA(())],
        input_output_aliases={2: 0},
        compiler_params=_cp(("arbitrary",)),
        name="moe_dispatch",
    )(dest, tok, xb0)


def _exp_up_kernel(be_ref, x_ref, wg_ref, wu_ref, o_ref, xs_ref):
    del be_ref

    @pl.when(pl.program_id(1) == 0)
    def _():
        xs_ref[...] = x_ref[...].astype(BF16)

    x = xs_ref[...]
    o_ref[...] = (_silu(_dot(x, wg_ref[...])) * _dot(x, wu_ref[...])).astype(o_ref.dtype)


def _moe_up(xb, block_e, w_gate, w_up, blk):
    n_slots, d = xb.shape
    ff = w_gate.shape[2]
    tn = _col_tile(ff)
    wspec = pl.BlockSpec((None, d, tn), lambda i, j, be: (be[i], 0, j))
    grid_spec = pltpu.PrefetchScalarGridSpec(
        num_scalar_prefetch=1,
        grid=(n_slots // blk, ff // tn),
        in_specs=[pl.BlockSpec((blk, d), lambda i, j, be: (i, 0)), wspec, wspec],
        out_specs=pl.BlockSpec((blk, tn), lambda i, j, be: (i, j)),
        scratch_shapes=[pltpu.VMEM((blk, d), BF16)],
    )
    return pl.pallas_call(
        _exp_up_kernel,
        grid_spec=grid_spec,
        out_shape=jax.ShapeDtypeStruct((n_slots, ff), BF16),
        compiler_params=_cp(("parallel", "arbitrary")),
        name="moe_expert_up",
    )(block_e, xb, w_gate, w_up)


def _exp_down_kernel(be_ref, h_ref, w_ref, o_ref):
    del be_ref
    o_ref[...] = _dot(h_ref[...], w_ref[...])


def _moe_down(hmid, block_e, w_down, blk):
    n_slots, ff = hmid.shape
    d = w_down.shape[2]
    grid_spec = pltpu.PrefetchScalarGridSpec(
        num_scalar_prefetch=1,
        grid=(n_slots // blk,),
        in_specs=[pl.BlockSpec((blk, ff), lambda i, be: (i, 0)),
                  pl.BlockSpec((None, ff, d), lambda i, be: (be[i], 0, 0))],
        out_specs=pl.BlockSpec((blk, d), lambda i, be: (i, 0)),
    )
    return pl.pallas_call(
        _exp_down_kernel,
        grid_spec=grid_spec,
        out_shape=jax.ShapeDtypeStruct((n_slots, d), F32),
        compiler_params=_cp(("arbitrary",)),
        name="moe_expert_down",
    )(block_e, hmid, w_down)


def _combine_kernel(dest_ref, yb_ref, info_ref, x_ref, g_ref, o_ref, y_ref, sem, *, tb):
    def start(g, _):
        for u in range(DMA_UNROLL):
            t = g * DMA_UNROLL + u
            _row_copy(yb_ref, y_ref.at[0], dest_ref[0, 0, 0, t], t, sem).start(priority=0)
            _row_copy(yb_ref, y_ref.at[1], dest_ref[0, 0, 1, t], t, sem).start(priority=1)
        return 0

    lax.fori_loop(0, tb // DMA_UNROLL, start, 0)
    for k in range(2):
        pltpu.make_async_copy(yb_ref.at[pl.ds(0, tb)], y_ref.at[k], sem).wait()
    info = info_ref[0]
    lane = lax.broadcasted_iota(jnp.int32, info.shape, 1)
    gate1 = jnp.sum(jnp.where(lane == 4, info, 0.0), axis=-1, keepdims=True)
    gate2 = jnp.sum(jnp.where(lane == 5, info, 0.0), axis=-1, keepdims=True)
    o_ref[0] = x_ref[0] + g_ref[0] * (gate1 * y_ref[0] + gate2 * y_ref[1])


def _moe_combine(yb, dest, info, x, gate):
    bn, l, d = x.shape
    tb = dest.shape[-1]
    nb = l // tb
    dest4 = dest.reshape(bn, nb, 2, tb)
    return pl.pallas_call(
        functools.partial(_combine_kernel, tb=tb),
        grid=(bn, nb),
        in_specs=[pl.BlockSpec((1, 1, 2, tb), lambda b, i: (b, i, 0, 0), memory_space=pltpu.SMEM),
                  pl.BlockSpec(memory_space=pl.ANY),
                  pl.BlockSpec((1, tb, LANES), lambda b, i: (b, i, 0)),
                  pl.BlockSpec((1, tb, d), lambda b, i: (b, i, 0)),
                  pl.BlockSpec((1, 1, d), lambda b, i: (b, 0, 0))],
        out_specs=pl.BlockSpec((1, tb, d), lambda b, i: (b, i, 0)),
        out_shape=jax.ShapeDtypeStruct(x.shape, F32),
        scratch_shapes=[pltpu.VMEM((2, tb, d), F32), pltpu.SemaphoreType.DMA(())],
        input_output_aliases={3: 0},
        compiler_params=_cp(("arbitrary", "arbitrary")),
        name="moe_combine",
    )(dest4, yb, info, x, gate)


def _moe_layer(x, gain, sc1p, sh, gate, w_router, w_gate, w_up, w_down, blk):
    bn, l, d = x.shape
    t = bn * l
    tok, info, cnt = _moe_route(x, gain, sc1p, sh, w_router)
    counts = cnt[0, :N_EXPERTS].astype(jnp.int32)
    padded = (counts + blk - 1) // blk * blk
    pad_end = jnp.cumsum(padded)
    pad_start = pad_end - padded
    n_blocks = -(-(2 * t) // blk) + N_EXPERTS
    block_e = jnp.minimum(
        jnp.sum((jnp.arange(n_blocks, dtype=jnp.int32)[:, None] * blk >= pad_end[None, :]).astype(jnp.int32), axis=1),
        N_EXPERTS - 1).astype(jnp.int32)
    info2 = info.reshape(t, LANES)
    ee = jnp.arange(N_EXPERTS, dtype=jnp.int32)[None, :]

    def slot(e_col, r_col):
        e = info2[:, e_col].astype(jnp.int32)
        return jnp.sum(jnp.where(e[:, None] == ee, pad_start[None, :], 0), axis=1) + info2[:, r_col].astype(jnp.int32)

    tb = min(256, l)
    dest = jnp.stack([slot(0, 2), slot(1, 3)], axis=0).reshape(2, t // tb, tb).transpose(1, 0, 2)
    xb = _moe_dispatch(tok.reshape(t, d), dest, n_blocks * blk)
    hmid = _moe_up(xb, block_e, w_gate, w_up, blk)
    yb = _moe_down(hmid, block_e, w_down, blk)
    return _moe_combine(yb, dest, info, x, gate)


def kernel(x, c, ctx, c_ctx, ada_w, ada_b, norm1_g, norm2_g, normf_g, cf_w_pw1, cf_b_pw1, cf_w_dw, cf_b_dw, cf_ln_g, cf_ln_b, cf_w_pw2, cf_b_pw2, hy_w_in, hy_b_in, hy_w_short, hy_b_short, hy_w_f1, hy_b_f1, hy_freq, hy_w_f2, hy_b_f2, hy_w_f3, hy_bias, hy_w_out, hy_b_out, hg_lb_logits, hg_w_in, hg_gn_g, hg_w_out, ffn_w_gate, ffn_w_up, ffn_w_down, moe_w_router, moe_w_gate, moe_w_up, moe_w_down):
    bn, s, d = x.shape
    cl = ctx.shape[1]
    depth = ada_w.shape[0]
    bf = lambda a: a.astype(BF16)

    cvecs = jnp.concatenate([c, c_ctx[None], jnp.zeros((8 - bn - 1, d), F32)], axis=0)
    mods = _ada_mod(cvecs, ada_w, ada_b)

    def mod_slices(i, ctx_rows):
        m = jnp.broadcast_to(mods[i, bn:bn + 1], (bn, N_MOD * d)) if ctx_rows else mods[i, :bn]
        parts = [m[:, k * d:(k + 1) * d].reshape(bn, 1, d) for k in range(N_MOD)]
        sh1, sc1, g1, sh2, sc2, g2 = parts
        return sh1, 1.0 + sc1, g1, sh2, 1.0 + sc2, g2

    p = jax.nn.softmax(hg_lb_logits.astype(F32), axis=0)
    lb_all = jnp.cumsum(p, axis=0) - p[0]

    x = _embed(x)
    xc = ctx

    for i in range(depth):
        last = i == depth - 1
        kind, j = i % 3, i // 3
        streams = [(x, mod_slices(i, False), False)]
        if not last:
            streams.append((xc, mod_slices(i, True), True))

        new = []
        if kind == 0:
            wa, wg = bf(cf_w_pw1[j][:, :d]), bf(cf_w_pw1[j][:, d:])
            ba, bg = cf_b_pw1[j][:d], cf_b_pw1[j][d:]
            w2 = bf(cf_w_pw2[j])
            for xs, (sh1, sc1, g1, _, _, _), _ in streams:
                u = _nmm(xs, norm1_g[i], sc1, sh1, [wa, wg], [ba, bg], epi="glu")
                new.append(_conformer_tail(u, cf_w_dw[j], cf_b_dw[j], cf_ln_g[j], cf_ln_b[j], w2, cf_b_pw2[j], xs, g1))
        elif kind == 1:
            order = hy_bias.shape[1]
            w_in_t = bf(hy_w_in[j].T)
            w_out = bf(hy_w_out[j])
            for xs, (sh1, sc1, g1, _, _, _), _ in streams:
                l = xs.shape[1]
                n2, r, n1, lp = _fft_plan(l)
                consts = _fft_consts(n1, n2)
                filt = _hyena_filters_t(l, lp, hy_w_f1[j], hy_b_f1[j], hy_freq[j], hy_w_f2[j], hy_b_f2[j],
                                        hy_w_f3[j], order, d)
                spec = _hyena_spectrum(filt, consts, n1, n2)
                ut = _nmm(xs, norm1_g[i], sc1, sh1, [w_in_t], [hy_b_in[j]], transposed=True)
                if lp > l:
                    ut = jnp.pad(ut, ((0, 0), (0, 0), (0, lp - l)))
                zt = _hyena_conv(ut, hy_w_short[j], hy_b_short[j], hy_bias[j], spec, consts, n1, n2, l)
                new.append(_mmres(zt, w_out, hy_b_out[j], xs, g1, trans_a=True))
        else:
            w_in = bf(hg_w_in[j])
            w_out = bf(hg_w_out[j])
            nh = d // HG_HEAD
            state = jnp.zeros((bn, 2, nh, HG_HEAD, HG_HEAD), F32)
            outs = {}
            for xs, (sh1, sc1, g1, _, _, _), is_ctx in ((xc, mod_slices(i, True), True), streams[0]):
                u5 = _nmm(xs, norm1_g[i], sc1, sh1, [w_in], [jnp.zeros((5 * d,), F32)])
                o_f, o_b, state = _hgrn_scan(u5, lb_all[i], state)
                if not (is_ctx and last):
                    outs[is_ctx] = _hgrn_out(o_f, o_b, u5, hg_gn_g[j], w_out, xs, g1)
            new = [outs[False]] + ([outs[True]] if not last else [])

        x = new[0]
        if not last:
            xc = new[1]
        streams = [(x, mod_slices(i, False), False)]
        if not last:
            streams.append((xc, mod_slices(i, True), True))

        new = []
        if i % 2 == 0:
            wg, wu, wd = bf(ffn_w_gate[i // 2]), bf(ffn_w_up[i // 2]), bf(ffn_w_down[i // 2])
            zb = jnp.zeros((wg.shape[1],), F32)
            for xs, (_, _, _, sh2, sc2, g2), _ in streams:
                hmid = _nmm(xs, norm2_g[i], sc2, sh2, [wg, wu], [zb, zb], epi="swiglu", out_dtype=BF16)
                new.append(_mmres(hmid, wd, jnp.zeros((d,), F32), xs, g2))
        else:
            wg, wu, wd = bf(moe_w_gate[i // 2]), bf(moe_w_up[i // 2]), bf(moe_w_down[i // 2])
            for xs, (_, _, _, sh2, sc2, g2), is_ctx in streams:
                new.append(_moe_layer(xs, norm2_g[i], sc2, sh2, g2, moe_w_router[i // 2], wg, wu, wd,
                                      blk=128 if is_ctx else 512))
        x = new[0]
        if not last:
            xc = new[1]

    return _final_norm(x, normf_g)


def _fnorm_kernel(x_ref, g_ref, o_ref):
    xv = x_ref[0]
    o_ref[0] = xv * lax.rsqrt(jnp.mean(xv * xv, axis=-1, keepdims=True) + EPS) * g_ref[...]


def _final_norm(x, g):
    bn, l, d = x.shape
    tm = min(512, l)
    return pl.pallas_call(
        _fnorm_kernel,
        grid=(bn, l // tm),
        in_specs=[pl.BlockSpec((1, tm, d), lambda b, i: (b, i, 0)), pl.BlockSpec((1, d), lambda b, i: (0, 0))],
        out_specs=pl.BlockSpec((1, tm, d), lambda b, i: (b, i, 0)),
        out_shape=jax.ShapeDtypeStruct(x.shape, F32),
        compiler_params=_cp(("parallel", "parallel")),
        name="final_norm",
    )(x, g.reshape(1, d))
```

```python
import functools
import math

import numpy as np
import jax
import jax.numpy as jnp
from jax import lax
from jax.experimental import pallas as pl
from jax.experimental.pallas import tpu as pltpu

F32 = jnp.float32
BF16 = jnp.bfloat16
EPS = 1e-6

GRID_W = 64
N_MOD = 6
CONV_WIDTH = 31
CONV_HALO = 16
HY_EMB = 33
HY_EMB_PAD = 40
HY_TARGET = 1e-2
HY_DECAY_MIN = math.log(HY_TARGET) / 1.5
HY_DECAY_MAX = math.log(HY_TARGET) / 0.3
HG_HEAD = 128
HG_CHUNK = 64
N_EXPERTS = 8
LANES = 128
DMA_UNROLL = 8
VMEM_LIMIT = 56 * 1024 * 1024


def _cp(sem, vmem=VMEM_LIMIT):
    return pltpu.CompilerParams(dimension_semantics=sem, vmem_limit_bytes=vmem)


def _dot(a, b):
    return jnp.dot(a, b, preferred_element_type=F32)


def _dot_nt(a, b):
    return lax.dot_general(a, b, (((1,), (1,)), ((), ())), preferred_element_type=F32)


def _dot_tn(a, b):
    return lax.dot_general(a, b, (((0,), (0,)), ((), ())), preferred_element_type=F32)


def _split3(a):
    h = a.astype(BF16)
    r = a - h.astype(F32)
    m = r.astype(BF16)
    l = (r - m.astype(F32)).astype(BF16)
    return h, m, l


def _dot_acc(a, b, dot=_dot):
    ah, am, _ = _split3(a)
    bh, bm, _ = _split3(b)
    return dot(ah, bh) + (dot(ah, bm) + dot(am, bh))


def _silu(x):
    return x * jax.nn.sigmoid(x)


def _ada_kernel(c_ref, w_ref, b_ref, o_ref):
    s = _silu(c_ref[...])
    o_ref[0] = _dot_acc(s, w_ref[0]) + b_ref[0]


def _ada_mod(cvecs, ada_w, ada_b):
    depth, d, n = ada_w.shape
    tn = 1024
    return pl.pallas_call(
        _ada_kernel,
        grid=(depth, n // tn),
        in_specs=[pl.BlockSpec((8, d), lambda i, j: (0, 0)),
                  pl.BlockSpec((1, d, tn), lambda i, j: (i, 0, j)),
                  pl.BlockSpec((1, 1, tn), lambda i, j: (i, 0, j))],
        out_specs=pl.BlockSpec((1, 8, tn), lambda i, j: (i, 0, j)),
        out_shape=jax.ShapeDtypeStruct((depth, 8, n), F32),
        compiler_params=_cp(("parallel", "parallel")),
        name="ada_mod",
    )(cvecs, ada_w, ada_b.reshape(depth, 1, n))


def _embed_kernel(x_ref, om_ref, o_ref, pos_ref, *, tm):
    i = pl.program_id(0)

    @pl.when(pl.program_id(1) == 0)
    def _():
        q = om_ref.shape[1]
        t = (lax.broadcasted_iota(jnp.int32, (tm, 1), 0) + i * tm).astype(F32)
        r = jnp.floor(t / GRID_W)
        col = t - r * GRID_W
        a = r * om_ref[...]
        b = col * om_ref[...]
        pos_ref[:, 0 * q:1 * q] = jnp.sin(a)
        pos_ref[:, 1 * q:2 * q] = jnp.cos(a)
        pos_ref[:, 2 * q:3 * q] = jnp.sin(b)
        pos_ref[:, 3 * q:4 * q] = jnp.cos(b)

    o_ref[0] = x_ref[0] + pos_ref[...]


def _embed(x):
    bn, s, d = x.shape
    tm = min(512, s)
    q = d // 4
    omega = (1.0 / (10000.0 ** (jnp.arange(q, dtype=F32) / q))).reshape(1, q)
    return pl.pallas_call(
        functools.partial(_embed_kernel, tm=tm),
        grid=(s // tm, bn),
        in_specs=[pl.BlockSpec((1, tm, d), lambda i, b: (b, i, 0)),
                  pl.BlockSpec((1, q), lambda i, b: (0, 0))],
        out_specs=pl.BlockSpec((1, tm, d), lambda i, b: (b, i, 0)),
        out_shape=jax.ShapeDtypeStruct(x.shape, F32),
        scratch_shapes=[pltpu.VMEM((tm, d), F32)],
        compiler_params=_cp(("parallel", "arbitrary")),
        name="pos_embed",
    )(x, omega)


def _norm_mod(x, g, sc1p, sh):
    ms = jnp.mean(x * x, axis=-1, keepdims=True)
    return (x * lax.rsqrt(ms + EPS) * g) * sc1p + sh


def _nmm_kernel(*refs, nw, epi, transposed):
    x_ref, g_ref, sc_ref, sh_ref = refs[:4]
    w_refs = refs[4:4 + nw]
    b_refs = refs[4 + nw:4 + 2 * nw]
    o_ref = refs[4 + 2 * nw]
    hs_ref = refs[5 + 2 * nw]

    @pl.when(pl.program_id(2) == 0)
    def _():
        hs_ref[...] = _norm_mod(x_ref[0], g_ref[...], sc_ref[0], sh_ref[0]).astype(BF16)

    h = hs_ref[...]
    if transposed:
        r = _dot_nt(w_refs[0][...], h) + b_refs[0][...]
    else:
        outs = [_dot(h, w[...]) + b[...] for w, b in zip(w_refs, b_refs)]
        if epi == "plain":
            r = outs[0]
        elif epi == "glu":
            r = outs[0] * jax.nn.sigmoid(outs[1])
        else:
            r = _silu(outs[0]) * outs[1]
    o_ref[0] = r.astype(o_ref.dtype)


def _col_tile(n, target=1024):
    best = None
    for t in range(256, n + 1, 256):
        if n % t == 0 and t <= target * 7 // 4:
            best = t
    return best if best is not None else n


def _nmm(x, gain, sc1p, sh, ws, bs, *, epi="plain", out_dtype=F32, transposed=False):
    bn, l, d = x.shape
    n = ws[0].shape[0] if transposed else ws[0].shape[1]
    tn = _col_tile(n)
    tm = min(1024 if len(ws) * tn <= 2048 else 512, l)
    nw = len(ws)
    if transposed:
        w_specs = [pl.BlockSpec((tn, d), lambda b, i, j: (j, 0))]
        b_specs = [pl.BlockSpec((tn, 1), lambda b, i, j: (j, 0))]
        bs = [bs[0].reshape(n, 1)]
        out_spec = pl.BlockSpec((1, tn, tm), lambda b, i, j: (b, j, i))
        out_shape = jax.ShapeDtypeStruct((bn, n, l), out_dtype)
    else:
        w_specs = [pl.BlockSpec((d, tn), lambda b, i, j: (0, j)) for _ in ws]
        b_specs = [pl.BlockSpec((1, tn), lambda b, i, j: (0, j)) for _ in ws]
        bs = [b.reshape(1, n) for b in bs]
        out_spec = pl.BlockSpec((1, tm, tn), lambda b, i, j: (b, i, j))
        out_shape = jax.ShapeDtypeStruct((bn, l, n), out_dtype)
    vec = pl.BlockSpec((1, 1, d), lambda b, i, j: (b, 0, 0))
    return pl.pallas_call(
        functools.partial(_nmm_kernel, nw=nw, epi=epi, transposed=transposed),
        grid=(bn, l // tm, n // tn),
        in_specs=[pl.BlockSpec((1, tm, d), lambda b, i, j: (b, i, 0)),
                  pl.BlockSpec((1, d), lambda b, i, j: (0, 0)), vec, vec] + w_specs + b_specs,
        out_specs=out_spec,
        out_shape=out_shape,
        scratch_shapes=[pltpu.VMEM((tm, d), BF16)],
        compiler_params=_cp(("parallel", "parallel", "arbitrary")),
        name="norm_mod_matmul_" + ("t" if transposed else epi),
    )(x, gain.reshape(1, d), sc1p, sh, *ws, *bs)


def _mmres_kernel(a_ref, w_ref, b_ref, x_ref, g_ref, o_ref, *, trans_a):
    a = a_ref[0]
    y = (_dot_tn(a, w_ref[...]) if trans_a else _dot(a, w_ref[...])) + b_ref[...]
    o_ref[0] = x_ref[0] + g_ref[0] * y


def _mmres(a, w, bias, x, gate, *, trans_a=False):
    bn, l, n = x.shape
    k = w.shape[0]
    tm = min(512, l)
    a_spec = (pl.BlockSpec((1, k, tm), lambda b, i: (b, 0, i)) if trans_a
              else pl.BlockSpec((1, tm, k), lambda b, i: (b, i, 0)))
    return pl.pallas_call(
        functools.partial(_mmres_kernel, trans_a=trans_a),
        grid=(bn, l // tm),
        in_specs=[a_spec,
                  pl.BlockSpec((k, n), lambda b, i: (0, 0)),
                  pl.BlockSpec((1, n), lambda b, i: (0, 0)),
                  pl.BlockSpec((1, tm, n), lambda b, i: (b, i, 0)),
                  pl.BlockSpec((1, 1, n), lambda b, i: (b, 0, 0))],
        out_specs=pl.BlockSpec((1, tm, n), lambda b, i: (b, i, 0)),
        out_shape=jax.ShapeDtypeStruct(x.shape, F32),
        input_output_aliases={3: 0},
        compiler_params=_cp(("parallel", "parallel")),
        name="matmul_residual" + ("_ta" if trans_a else ""),
    )(a, w, bias.reshape(1, n), x, gate)


def _cf2_kernel(prev_ref, cur_ref, next_ref, wdw_ref, bdw_ref, lng_ref, lnb_ref, w2_ref, b2_ref,
                x_ref, g_ref, o_ref, buf_ref, sh_ref, cv_ref, *, tm):
    i = pl.program_id(1)
    nt = pl.num_programs(1)
    h = CONV_HALO
    buf_ref[0:h] = jnp.where(i > 0, prev_ref[0], 0.0)
    buf_ref[h:h + tm] = cur_ref[0]
    buf_ref[h + tm:2 * h + tm] = jnp.where(i < nt - 1, next_ref[0], 0.0)
    span = sh_ref.shape[1]
    for s in range(8):
        sh_ref[s] = buf_ref[s:s + span]
    off = h - (CONV_WIDTH - 1) // 2
    rc = 8
    for c in range(tm // rc):
        acc = None
        for k in range(CONV_WIDTH):
            q, s = divmod(off + k, 8)
            term = sh_ref[s, c * rc + 8 * q:c * rc + 8 * q + rc] * wdw_ref[k]
            acc = term if acc is None else acc + term
        cv_ref[c * rc:(c + 1) * rc] = acc + bdw_ref[...]
    u = cv_ref[...]
    mu = jnp.mean(u, axis=-1, keepdims=True)
    uc = u - mu
    var = jnp.mean(uc * uc, axis=-1, keepdims=True)
    v = _silu(uc * lax.rsqrt(var + EPS) * lng_ref[...] + lnb_ref[...])
    y = _dot(v.astype(BF16), w2_ref[...]) + b2_ref[...]
    o_ref[0] = x_ref[0] + g_ref[0] * y


def _conformer_tail(u, w_dw, b_dw, ln_g, ln_b, w2, b2, x, gate):
    bn, l, d = x.shape
    tm = min(256, l)
    h = CONV_HALO
    nh = l // h
    r = tm // h
    row = lambda a: a.reshape(1, d)
    full = lambda shp: pl.BlockSpec(shp, lambda b, i: (0,) * len(shp))
    return pl.pallas_call(
        functools.partial(_cf2_kernel, tm=tm),
        grid=(bn, l // tm),
        in_specs=[pl.BlockSpec((1, h, d), lambda b, i: (b, jnp.maximum(i * r - 1, 0), 0)),
                  pl.BlockSpec((1, tm, d), lambda b, i: (b, i, 0)),
                  pl.BlockSpec((1, h, d), lambda b, i: (b, jnp.minimum((i + 1) * r, nh - 1), 0)),
                  full((CONV_WIDTH, 8, d)), full((1, d)), full((1, d)), full((1, d)),
                  full((d, d)), full((1, d)),
                  pl.BlockSpec((1, tm, d), lambda b, i: (b, i, 0)),
                  pl.BlockSpec((1, 1, d), lambda b, i: (b, 0, 0))],
        out_specs=pl.BlockSpec((1, tm, d), lambda b, i: (b, i, 0)),
        out_shape=jax.ShapeDtypeStruct(x.shape, F32),
        scratch_shapes=[pltpu.VMEM((tm + 2 * h, d), F32),
                        pltpu.VMEM((8, tm + 2 * h - 8, d), F32),
                        pltpu.VMEM((tm, d), F32)],
        input_output_aliases={9: 0},
        compiler_params=_cp(("parallel", "parallel")),
        name="conformer_tail",
    )(u, u, u, jnp.broadcast_to(w_dw[:, None, :], (CONV_WIDTH, 8, d)), row(b_dw), row(ln_g), row(ln_b),
      w2, row(b2), x, gate)


def _fft_plan(l):
    n2 = 256 if l >= 4096 else 128
    r = max(16, -(-l // n2))
    nk = -(-(r + 1) // 16) * 16
    return n2, r, nk, r * n2


def _fft_channels(r, n2, d):
    return max(8, min(64, d, (128 * 1024) // (r * n2)))


def _fft_consts(r, nk, n2):
    n1 = 2 * r
    n = n1 * n2
    k1 = np.arange(nk, dtype=np.float64)[:, None]
    live = (k1 <= r).astype(np.float64)
    a1 = 2.0 * np.pi * k1 * np.arange(r, dtype=np.float64)[None, :] / n1
    fc1 = np.concatenate([np.cos(a1) * live, -np.sin(a1) * live], axis=0)
    at = 2.0 * np.pi * k1 * np.arange(n2, dtype=np.float64)[None, :] / n
    tw = np.stack([np.cos(at), -np.sin(at)], axis=0)
    a2 = 2.0 * np.pi * np.outer(np.arange(n2, dtype=np.float64), np.arange(n2, dtype=np.float64)) / n2
    f2 = np.concatenate([np.cos(a2), -np.sin(a2)], axis=1)
    g2 = np.concatenate([np.cos(a2), np.sin(a2)], axis=1)
    wk = (live * np.where((k1 == 0) | (k1 == r), 1.0, 2.0)).T
    ai = 2.0 * np.pi * np.arange(r, dtype=np.float64)[:, None] * k1.T / n1
    c1i = np.concatenate([np.cos(ai) * wk, -np.sin(ai) * wk], axis=1) / n
    return (jnp.asarray(fc1, BF16), jnp.asarray(tw, F32), jnp.asarray(f2, BF16),
            jnp.asarray(g2, BF16), jnp.asarray(c1i, BF16))


def _fft_fwd_rows(z, fc1, tr, ti):
    n1 = tr.shape[0]
    a = _dot(fc1, z.astype(BF16))
    ar, ai = a[:n1], a[n1:]
    return ar * tr - ai * ti, ar * ti + ai * tr


def _cplx_rows(p, ch, n1, n2):
    p4 = p.reshape(ch, 2, n1, 2 * n2)
    return p4[:, 0, :, :n2] - p4[:, 1, :, n2:], p4[:, 0, :, n2:] + p4[:, 1, :, :n2]


def _hyfilt_kernel(w1_ref, b1_ref, fr_ref, w2_ref, b2_ref, fb_ref, w3_ref, dl_ref, o_ref, hid_ref, *, l, lp):
    lane = lax.broadcasted_iota(jnp.int32, (1, lp), 1)
    pos = lane.astype(F32)
    t = pos / (l - 1)

    @pl.when((pl.program_id(0) == 0) & (pl.program_id(1) == 0))
    def _():
        w = (2.0 * math.pi) * pos / l
        arg = fb_ref[...] * w
        row = lax.broadcasted_iota(jnp.int32, (HY_EMB_PAD, 1), 0)
        bands = (HY_EMB - 1) // 2
        z = jnp.where(row == 0, t,
                      jnp.where(row <= bands, jnp.cos(arg),
                                jnp.where(row < HY_EMB, -jnp.sin(arg), 0.0)))
        fr = fr_ref[...]
        hid = jnp.sin(fr * (_dot_acc(w1_ref[...], z) + b1_ref[...]))
        for j in range(w2_ref.shape[0]):
            hid = jnp.sin(fr * (_dot_acc(w2_ref[j], hid) + b2_ref[j]))
        hid_ref[...] = hid

    decay = jnp.exp(-t * dl_ref[...])
    hid = hid_ref[...]
    valid = lane < l
    f0 = jnp.where(valid, _dot_acc(w3_ref[0, 0], hid) * decay, 0.0)
    f1 = jnp.where(valid & (lane > 0), _dot_acc(w3_ref[0, 1], hid) * decay, 0.0)
    norm = jnp.sum(jnp.abs(f0), axis=-1, keepdims=True) + jnp.sum(jnp.abs(f1), axis=-1, keepdims=True)
    o_ref[0, 0] = f0 / norm
    o_ref[0, 1] = f1 / norm


def _hyena_filters_t(l, lp, w_f1, b_f1, freq, w_f2, b_f2, w_f3, order, d):
    fw = w_f1.shape[1]
    bands = (HY_EMB - 1) // 2
    fb = jnp.linspace(1e-4, bands - 1, bands, dtype=F32)
    fbcol = jnp.concatenate([jnp.zeros((1,), F32), fb, fb, jnp.zeros((HY_EMB_PAD - HY_EMB,), F32)]).reshape(-1, 1)
    w1t = jnp.pad(w_f1.T, ((0, 0), (0, HY_EMB_PAD - HY_EMB)))
    w2t = jnp.swapaxes(w_f2, 1, 2)
    w3t = w_f3.T.reshape(order, 2, d, fw)
    deltas = jnp.abs(jnp.linspace(HY_DECAY_MIN, HY_DECAY_MAX, d, dtype=F32)).reshape(d, 1)
    tr = 32
    c2 = lambda shp: pl.BlockSpec(shp, lambda o, c: (0,) * len(shp))
    return pl.pallas_call(
        functools.partial(_hyfilt_kernel, l=l, lp=lp),
        grid=(order, d // tr),
        in_specs=[c2((fw, HY_EMB_PAD)), c2((fw, 1)), c2((fw, 1)), c2(w2t.shape), c2((w_f2.shape[0], fw, 1)),
                  c2((HY_EMB_PAD, 1)),
                  pl.BlockSpec((1, 2, tr, fw), lambda o, c: (o, 0, c, 0)),
                  pl.BlockSpec((tr, 1), lambda o, c: (c, 0))],
        out_specs=pl.BlockSpec((1, 2, tr, lp), lambda o, c: (o, 0, c, 0)),
        out_shape=jax.ShapeDtypeStruct((order, 2, d, lp), F32),
        scratch_shapes=[pltpu.VMEM((fw, lp), F32)],
        compiler_params=_cp(("arbitrary", "arbitrary")),
        name="hyena_filter",
    )(w1t, b_f1.reshape(fw, 1), freq.reshape(fw, 1), w2t, b_f2.reshape(-1, fw, 1), fbcol, w3t, deltas)


def _hyspec_kernel(f_ref, fc1_ref, tw_ref, f2_ref, o_ref, ys_ref, *, dc, n1, n2):
    tr, ti = tw_ref[0], tw_ref[1]
    fc1 = fc1_ref[...]

    def stage1(c, _):
        for s in range(2):
            yr, yi = _fft_fwd_rows(f_ref[0, s, c], fc1, tr, ti)
            ys_ref[s, c, :n1] = yr.astype(BF16)
            ys_ref[s, c, n1:] = yi.astype(BF16)
        return 0

    for c in range(dc):
        stage1(c, 0)
    zs = []
    for s in range(2):
        p = _dot(ys_ref[s].reshape(dc * 2 * n1, n2), f2_ref[...])
        zs.append(_cplx_rows(p, dc, n1, n2))
    o_ref[0, 0] = zs[0][0] + zs[1][0]
    o_ref[0, 1] = zs[0][1] - zs[1][1]


def _hyena_spectrum(filt, consts, r, n1, n2):
    order, _, d, lp = filt.shape
    dc = _fft_channels(r, n2, d)
    fc1, tw, f2, _, _ = consts
    c2 = lambda a: pl.BlockSpec(a.shape, lambda o, c: (0,) * a.ndim)
    return pl.pallas_call(
        functools.partial(_hyspec_kernel, dc=dc, n1=n1, n2=n2),
        grid=(order, d // dc),
        in_specs=[pl.BlockSpec((1, 2, dc, r, n2), lambda o, c: (o, 0, c, 0, 0)), c2(fc1), c2(tw), c2(f2)],
        out_specs=pl.BlockSpec((1, 2, dc, n1, n2), lambda o, c: (o, 0, c, 0, 0)),
        out_shape=jax.ShapeDtypeStruct((order, 2, d, n1, n2), F32),
        scratch_shapes=[pltpu.VMEM((2, dc, 2 * n1, n2), BF16)],
        compiler_params=_cp(("parallel", "parallel")),
        name="hyena_spectrum",
    )(filt.reshape(order, 2, d, r, n2), fc1, tw, f2)


def _hyconv_kernel(wsh_ref, hb_ref, v_ref, x1_ref, x2_ref, h_ref, fc1_ref, tw_ref, f2_ref, g2_ref, c1i_ref,
                   o_ref, cs_ref, ys_ref, vs_ref, *, dc, ch, r, n1, n2, d, l, order):
    cb = pl.program_id(0)
    tr, ti = tw_ref[0], tw_ref[1]
    row = lax.broadcasted_iota(jnp.int32, (r, n2), 0)
    lane = lax.broadcasted_iota(jnp.int32, (r, n2), 1)
    valid = (row * n2 + lane) < l

    def short_conv(c, _):
        chan = cb * dc + c
        for s, ref in enumerate((v_ref, x1_ref, x2_ref)):
            u = ref[0, 0, c]
            idx = s * d + chan
            a = pltpu.roll(u, 1, axis=1)
            prev = jnp.where(lane == 0, jnp.where(row == 0, 0.0, pltpu.roll(a, 1, axis=0)), a)
            b = pltpu.roll(u, n2 - 1, axis=1)
            nxt = jnp.where(lane == n2 - 1, jnp.where(row == r - 1, 0.0, pltpu.roll(b, r - 1, axis=0)), b)
            y = wsh_ref[0, idx] * prev + wsh_ref[1, idx] * u + wsh_ref[2, idx] * nxt + wsh_ref[3, idx]
            cs_ref[s, c] = jnp.where(valid, y, 0.0)
        return 0

    for c in range(dc):
        short_conv(c, 0)

    for n in range(order):
        def stage1(c, _):
            yr, yi = _fft_fwd_rows(cs_ref[0, c], fc1_ref[...], tr, ti)
            ys_ref[c, :n1] = yr.astype(BF16)
            ys_ref[c, n1:] = yi.astype(BF16)
            return 0

        for c in range(dc):
            stage1(c, 0)

        for c0 in range(0, dc, ch):
            p = _dot(ys_ref[c0:c0 + ch].reshape(ch * 2 * n1, n2), f2_ref[...])
            zr, zi = _cplx_rows(p, ch, n1, n2)
            hr, hi = h_ref[n, 0, c0:c0 + ch], h_ref[n, 1, c0:c0 + ch]
            vs_ref[c0:c0 + ch, :n1] = (zr * hr - zi * hi).astype(BF16)
            vs_ref[c0:c0 + ch, n1:] = (zr * hi + zi * hr).astype(BF16)
            q = _dot(vs_ref[c0:c0 + ch].reshape(ch * 2 * n1, n2), g2_ref[...])
            ur, ui = _cplx_rows(q, ch, n1, n2)
            ys_ref[c0:c0 + ch, :n1] = (ur * tr + ui * ti).astype(BF16)
            ys_ref[c0:c0 + ch, n1:] = (ui * tr - ur * ti).astype(BF16)

        def stage4(c, _):
            chan = cb * dc + c
            conv = _dot(c1i_ref[...], ys_ref[c])
            z = cs_ref[n + 1, c] * (conv + hb_ref[n, chan] * cs_ref[0, c])
            if n == order - 1:
                o_ref[0, c] = z.astype(o_ref.dtype)
            else:
                cs_ref[0, c] = z
            return 0

        for c in range(dc):
            stage4(c, 0)


def _hyena_conv(ut, w_short, b_short, hy_bias, spec, consts, r, n1, n2, l):
    bn, d3, lp = ut.shape
    d = d3 // 3
    order = hy_bias.shape[0]
    dc = _fft_channels(r, n2, d)
    ch = max(c for c in range(1, dc + 1) if dc % c == 0 and (c == 1 or c * 2 * n1 <= 1536))
    fc1, tw, f2, g2, c1i = consts
    wsh = jnp.concatenate([w_short, b_short.reshape(1, d3)], axis=0)
    u5 = ut.reshape(bn, 3, d, r, n2)
    cst = lambda a: pl.BlockSpec(a.shape, lambda c, b, *_: (0,) * a.ndim)
    stream = lambda s: pl.BlockSpec((1, 1, dc, r, n2), lambda c, b, *_: (b, s, c, 0, 0))
    grid_spec = pltpu.PrefetchScalarGridSpec(
        num_scalar_prefetch=2,
        grid=(d // dc, bn),
        in_specs=[stream(0), stream(1), stream(2),
                  pl.BlockSpec((order, 2, dc, n1, n2), lambda c, b, *_: (0, 0, c, 0, 0)),
                  cst(fc1), cst(tw), cst(f2), cst(g2), cst(c1i)],
        out_specs=pl.BlockSpec((1, dc, r, n2), lambda c, b, *_: (b, c, 0, 0)),
        scratch_shapes=[pltpu.VMEM((3, dc, r, n2), F32),
                        pltpu.VMEM((dc, 2 * n1, n2), BF16),
                        pltpu.VMEM((dc, 2 * n1, n2), BF16)],
    )
    out = pl.pallas_call(
        functools.partial(_hyconv_kernel, dc=dc, ch=ch, r=r, n1=n1, n2=n2, d=d, l=l, order=order),
        grid_spec=grid_spec,
        out_shape=jax.ShapeDtypeStruct((bn, d, r, n2), BF16),
        compiler_params=_cp(("parallel", "arbitrary")),
        name="hyena_conv",
    )(wsh, hy_bias, u5, u5, u5, spec, fc1, tw, f2, g2, c1i)
    return out.reshape(bn, d, lp)


def _scan_kernel(qf_ref, vf_ref, zf_ref, qb_ref, vb_ref, zb_ref, lb_ref, tri_ref, s0_ref,
                 of_ref, ob_ref, sf_ref, st_ref, *, tt, hp):
    i = pl.program_id(2)
    ck = HG_CHUNK
    nck = tt // ck

    @pl.when(i == 0)
    def _():
        st_ref[...] = s0_ref[0]

    for h in range(hp):
        hs = slice(h * HG_HEAD, (h + 1) * HG_HEAD)
        lb = lb_ref[:, hs]
        for di, (q_ref, v_ref, z_ref, o_ref) in enumerate(((qf_ref, vf_ref, zf_ref, of_ref),
                                                           (qb_ref, vb_ref, zb_ref, ob_ref))):
            tri = tri_ref[di]
            f = lb + (1.0 - lb) * jax.nn.sigmoid(z_ref[0, :, hs])
            g = jnp.log(f)
            k = 1.0 - f
            gh, gm, _ = _split3(g)
            b = _dot(tri, gh) + _dot(tri, gm)
            tot = [b[c * ck + ck - 1:c * ck + ck] if di == 0 else b[c * ck:c * ck + 1] for c in range(nck)]
            btot = jnp.concatenate([jnp.broadcast_to(r, (ck, HG_HEAD)) for r in tot], axis=0)
            qd = (_silu(q_ref[0, :, hs]) * jnp.exp(b)).astype(BF16)
            ki = (k * jnp.exp(-b)).astype(BF16)
            ke = (k * jnp.exp(btot - b)).astype(BF16)
            vb = v_ref[0, :, hs].astype(BF16)
            att = jnp.where(tri != 0, _dot_nt(qd, ki), 0.0)
            o_intra = _dot(att.astype(BF16), vb)
            kv = [_dot_tn(vb[c * ck:(c + 1) * ck], ke[c * ck:(c + 1) * ck]) for c in range(nck)]
            st = st_ref[di, h]
            inter = [None] * nck
            for c in (range(nck) if di == 0 else reversed(range(nck))):
                inter[c] = _dot_nt(qd[c * ck:(c + 1) * ck], st.astype(BF16))
                st = st * jnp.exp(tot[c]) + kv[c]
            st_ref[di, h] = st
            o_ref[0, :, hs] = o_intra + jnp.concatenate(inter, axis=0)

    @pl.when(i == pl.num_programs(2) - 1)
    def _():
        sf_ref[0] = st_ref[...]


def _hgrn_scan(u5, lb, s0):
    bn, l, d5 = u5.shape
    d = d5 // 5
    nh = d // HG_HEAD
    hp = 4 if nh % 4 == 0 else 1
    hw = hp * HG_HEAD
    ng = nh // hp
    tt = min(256, l)
    nt = l // tt
    col = lambda s, rev: pl.BlockSpec(
        (1, tt, hw), (lambda b, h, i: (b, nt - 1 - i, s * ng + h)) if rev else (lambda b, h, i: (b, i, s * ng + h)))
    st_spec = pl.BlockSpec((1, 2, hp, HG_HEAD, HG_HEAD), lambda b, h, i: (b, 0, h, 0, 0))
    t_idx = np.arange(tt)
    same = (t_idx[:, None] // HG_CHUNK) == (t_idx[None, :] // HG_CHUNK)
    tri = jnp.asarray(np.stack([same & (t_idx[:, None] >= t_idx[None, :]),
                                same & (t_idx[:, None] <= t_idx[None, :])]), BF16)
    return pl.pallas_call(
        functools.partial(_scan_kernel, tt=tt, hp=hp),
        grid=(bn, ng, nt),
        in_specs=[col(0, False), col(1, False), col(3, False), col(0, True), col(1, True), col(4, True),
                  pl.BlockSpec((1, hw), lambda b, h, i: (0, h)),
                  pl.BlockSpec((2, tt, tt), lambda b, h, i: (0, 0, 0)), st_spec],
        out_specs=[col(0, False), col(0, True), st_spec],
        out_shape=[jax.ShapeDtypeStruct((bn, l, d), F32), jax.ShapeDtypeStruct((bn, l, d), F32),
                   jax.ShapeDtypeStruct(s0.shape, F32)],
        scratch_shapes=[pltpu.VMEM((2, hp, HG_HEAD, HG_HEAD), F32)],
        compiler_params=_cp(("parallel", "parallel", "arbitrary")),
        name="hgrn2_scan",
    )(u5, u5, u5, u5, u5, u5, lb.reshape(1, d), tri, s0)


def _hgout_kernel(of_ref, ob_ref, og_ref, gn_ref, w_ref, x_ref, g_ref, o_ref, a_ref, *, nh):
    for h in range(nh):
        sl = slice(h * HG_HEAD, (h + 1) * HG_HEAD)
        o = of_ref[0, :, sl] + ob_ref[0, :, sl]
        o = o * lax.rsqrt(jnp.mean(o * o, axis=-1, keepdims=True) + EPS) * gn_ref[:, sl]
        a_ref[:, sl] = (o * _silu(og_ref[0, :, sl])).astype(BF16)
    o_ref[0] = x_ref[0] + g_ref[0] * _dot(a_ref[...], w_ref[...])


def _hgrn_out(o_f, o_b, u5, gn_g, w_out, x, gate):
    bn, l, d = x.shape
    nh = d // HG_HEAD
    tm = min(512, l)
    blk = lambda c: pl.BlockSpec((1, tm, d), lambda b, i: (b, i, c))
    return pl.pallas_call(
        functools.partial(_hgout_kernel, nh=nh),
        grid=(bn, l // tm),
        in_specs=[blk(0), blk(0), blk(2),
                  pl.BlockSpec((1, d), lambda b, i: (0, 0)),
                  pl.BlockSpec((d, d), lambda b, i: (0, 0)),
                  blk(0), pl.BlockSpec((1, 1, d), lambda b, i: (b, 0, 0))],
        out_specs=blk(0),
        out_shape=jax.ShapeDtypeStruct(x.shape, F32),
        scratch_shapes=[pltpu.VMEM((tm, d), BF16)],
        input_output_aliases={5: 0},
        compiler_params=_cp(("parallel", "parallel")),
        name="hgrn2_out",
    )(o_f, o_b, u5, gn_g.reshape(1, d), w_out, x, gate)


def _route_kernel(x_ref, g_ref, sc_ref, sh_ref, wr_ref, tok_ref, info_ref, cnt_ref, run_ref, *, tm):
    first = (pl.program_id(0) == 0) & (pl.program_id(1) == 0)

    @pl.when(first)
    def _():
        run_ref[...] = jnp.zeros_like(run_ref)

    h = _norm_mod(x_ref[0], g_ref[...], sc_ref[0], sh_ref[0])
    tok_ref[0] = h
    logits = _dot_acc(h, wr_ref[...])
    lane = lax.broadcasted_iota(jnp.int32, (tm, LANES), 1)
    lanef = lane.astype(F32)
    neg = jnp.float32(-jnp.inf)
    lg = jnp.where(lane < N_EXPERTS, logits, neg)
    m1 = jnp.max(lg, axis=-1, keepdims=True)
    i1 = jnp.min(jnp.where(lg == m1, lanef, float(LANES)), axis=-1, keepdims=True)
    lg2 = jnp.where(lanef == i1, neg, lg)
    m2 = jnp.max(lg2, axis=-1, keepdims=True)
    i2 = jnp.min(jnp.where(lg2 == m2, lanef, float(LANES)), axis=-1, keepdims=True)
    e2 = jnp.exp(m2 - m1)
    gate1 = 1.0 / (1.0 + e2)
    gate2 = e2 / (1.0 + e2)
    oh1 = (lanef == i1).astype(F32)
    oh2 = (lanef == i2).astype(F32)
    both = oh1 + oh2
    rr = lax.broadcasted_iota(jnp.int32, (tm, tm), 0)
    cc = lax.broadcasted_iota(jnp.int32, (tm, tm), 1)
    before = _dot((rr > cc).astype(BF16), both.astype(BF16)) + run_ref[...]
    r1 = jnp.sum(before * oh1, axis=-1, keepdims=True)
    r2 = jnp.sum(before * oh2, axis=-1, keepdims=True)
    run = run_ref[...] + jnp.sum(both, axis=0, keepdims=True)
    run_ref[...] = run
    cnt_ref[...] = jnp.broadcast_to(run, cnt_ref.shape)
    info = jnp.where(lane == 0, i1, jnp.where(lane == 1, i2, jnp.where(lane == 2, r1, jnp.where(
        lane == 3, r2, jnp.where(lane == 4, gate1, jnp.where(lane == 5, gate2, 0.0))))))
    info_ref[0] = info


def _moe_route(x, gain, sc1p, sh, w_router):
    bn, l, d = x.shape
    tm = min(256, l)
    wr = jnp.pad(w_router, ((0, 0), (0, LANES - w_router.shape[1])))
    vec = pl.BlockSpec((1, 1, d), lambda b, i: (b, 0, 0))
    return pl.pallas_call(
        functools.partial(_route_kernel, tm=tm),
        grid=(bn, l // tm),
        in_specs=[pl.BlockSpec((1, tm, d), lambda b, i: (b, i, 0)),
                  pl.BlockSpec((1, d), lambda b, i: (0, 0)), vec, vec,
                  pl.BlockSpec((d, LANES), lambda b, i: (0, 0))],
        out_specs=[pl.BlockSpec((1, tm, d), lambda b, i: (b, i, 0)),
                   pl.BlockSpec((1, tm, LANES), lambda b, i: (b, i, 0)),
                   pl.BlockSpec((8, LANES), lambda b, i: (0, 0))],
        out_shape=[jax.ShapeDtypeStruct((bn, l, d), F32),
                   jax.ShapeDtypeStruct((bn, l, LANES), F32),
                   jax.ShapeDtypeStruct((8, LANES), F32)],
        scratch_shapes=[pltpu.VMEM((1, LANES), F32)],
        compiler_params=_cp(("arbitrary", "arbitrary")),
        name="moe_route",
    )(x, gain.reshape(1, d), sc1p, sh, wr)


def _row_copy(src, dst, s, t, sem):
    return pltpu.make_async_copy(src.at[pl.ds(s, 1)], dst.at[pl.ds(t, 1)], sem)


def _dispatch_kernel(dest_ref, tok_ref, xb_in_ref, xb_ref, sem, *, tb):
    del xb_in_ref

    def start(g, _):
        for u in range(DMA_UNROLL):
            t = g * DMA_UNROLL + u
            _row_copy(tok_ref, xb_ref, t, dest_ref[0, 0, t], sem).start(priority=0)
            _row_copy(tok_ref, xb_ref, t, dest_ref[0, 1, t], sem).start(priority=1)
        return 0

    lax.fori_loop(0, tb // DMA_UNROLL, start, 0)
    for _ in range(2):
        pltpu.make_async_copy(tok_ref, xb_ref.at[pl.ds(0, tb)], sem).wait()


def _moe_dispatch(tok, dest, n_slots):
    t, d = tok.shape
    nb, _, tb = dest.shape
    xb0 = jnp.zeros((n_slots, d), F32)
    return pl.pallas_call(
        functools.partial(_dispatch_kernel, tb=tb),
        grid=(nb,),
        in_specs=[pl.BlockSpec((1, 2, tb), lambda i: (i, 0, 0), memory_space=pltpu.SMEM),
                  pl.BlockSpec((tb, d), lambda i: (i, 0)),
                  pl.BlockSpec(memory_space=pl.ANY)],
        out_specs=pl.BlockSpec(memory_space=pl.ANY),
        out_shape=jax.ShapeDtypeStruct((n_slots, d), F32),
        scratch_shapes=[pltpu.SemaphoreType.DMA(())],
        input_output_aliases={2: 0},
        compiler_params=_cp(("arbitrary",)),
        name="moe_dispatch",
    )(dest, tok, xb0)


def _exp_up_kernel(be_ref, x_ref, wg_ref, wu_ref, o_ref, xs_ref):
    del be_ref

    @pl.when(pl.program_id(1) == 0)
    def _():
        xs_ref[...] = x_ref[...].astype(BF16)

    x = xs_ref[...]
    o_ref[...] = (_silu(_dot(x, wg_ref[...])) * _dot(x, wu_ref[...])).astype(o_ref.dtype)


def _moe_up(xb, block_e, w_gate, w_up, blk):
    n_slots, d = xb.shape
    ff = w_gate.shape[2]
    tn = _col_tile(ff)
    wspec = pl.BlockSpec((None, d, tn), lambda i, j, be: (be[i], 0, j))
    grid_spec = pltpu.PrefetchScalarGridSpec(
        num_scalar_prefetch=1,
        grid=(n_slots // blk, ff // tn),
        in_specs=[pl.BlockSpec((blk, d), lambda i, j, be: (i, 0)), wspec, wspec],
        out_specs=pl.BlockSpec((blk, tn), lambda i, j, be: (i, j)),
        scratch_shapes=[pltpu.VMEM((blk, d), BF16)],
    )
    return pl.pallas_call(
        _exp_up_kernel,
        grid_spec=grid_spec,
        out_shape=jax.ShapeDtypeStruct((n_slots, ff), BF16),
        compiler_params=_cp(("parallel", "arbitrary")),
        name="moe_expert_up",
    )(block_e, xb, w_gate, w_up)


def _exp_down_kernel(be_ref, h_ref, w_ref, o_ref):
    del be_ref
    o_ref[...] = _dot(h_ref[...], w_ref[...])


def _moe_down(hmid, block_e, w_down, blk):
    n_slots, ff = hmid.shape
    d = w_down.shape[2]
    grid_spec = pltpu.PrefetchScalarGridSpec(
        num_scalar_prefetch=1,
        grid=(n_slots // blk,),
        in_specs=[pl.BlockSpec((blk, ff), lambda i, be: (i, 0)),
                  pl.BlockSpec((None, ff, d), lambda i, be: (be[i], 0, 0))],
        out_specs=pl.BlockSpec((blk, d), lambda i, be: (i, 0)),
    )
    return pl.pallas_call(
        _exp_down_kernel,
        grid_spec=grid_spec,
        out_shape=jax.ShapeDtypeStruct((n_slots, d), F32),
        compiler_params=_cp(("arbitrary",)),
        name="moe_expert_down",
    )(block_e, hmid, w_down)


def _combine_kernel(dest_ref, yb_ref, info_ref, x_ref, g_ref, fg_ref, o_ref, y_ref, sem, *, tb, final):
    def start(g, _):
        for u in range(DMA_UNROLL):
            t = g * DMA_UNROLL + u
            _row_copy(yb_ref, y_ref.at[0], dest_ref[0, 0, 0, t], t, sem).start(priority=0)
            _row_copy(yb_ref, y_ref.at[1], dest_ref[0, 0, 1, t], t, sem).start(priority=1)
        return 0

    lax.fori_loop(0, tb // DMA_UNROLL, start, 0)
    for k in range(2):
        pltpu.make_async_copy(yb_ref.at[pl.ds(0, tb)], y_ref.at[k], sem).wait()
    info = info_ref[0]
    lane = lax.broadcasted_iota(jnp.int32, info.shape, 1)
    gate1 = jnp.sum(jnp.where(lane == 4, info, 0.0), axis=-1, keepdims=True)
    gate2 = jnp.sum(jnp.where(lane == 5, info, 0.0), axis=-1, keepdims=True)
    xn = x_ref[0] + g_ref[0] * (gate1 * y_ref[0] + gate2 * y_ref[1])
    if final:
        xn = xn * lax.rsqrt(jnp.mean(xn * xn, axis=-1, keepdims=True) + EPS) * fg_ref[...]
    o_ref[0] = xn


def _moe_combine(yb, dest, info, x, gate, final_g):
    bn, l, d = x.shape
    tb = dest.shape[-1]
    nb = l // tb
    dest4 = dest.reshape(bn, nb, 2, tb)
    final = final_g is not None
    fg = (final_g if final else jnp.ones((d,), F32)).reshape(1, d)
    return pl.pallas_call(
        functools.partial(_combine_kernel, tb=tb, final=final),
        grid=(bn, nb),
        in_specs=[pl.BlockSpec((1, 1, 2, tb), lambda b, i: (b, i, 0, 0), memory_space=pltpu.SMEM),
                  pl.BlockSpec(memory_space=pl.ANY),
                  pl.BlockSpec((1, tb, LANES), lambda b, i: (b, i, 0)),
                  pl.BlockSpec((1, tb, d), lambda b, i: (b, i, 0)),
                  pl.BlockSpec((1, 1, d), lambda b, i: (b, 0, 0)),
                  pl.BlockSpec((1, d), lambda b, i: (0, 0))],
        out_specs=pl.BlockSpec((1, tb, d), lambda b, i: (b, i, 0)),
        out_shape=jax.ShapeDtypeStruct(x.shape, F32),
        scratch_shapes=[pltpu.VMEM((2, tb, d), F32), pltpu.SemaphoreType.DMA(())],
        input_output_aliases={3: 0},
        compiler_params=_cp(("arbitrary", "arbitrary")),
        name="moe_combine",
    )(dest4, yb, info, x, gate, fg)


def _moe_layer(x, gain, sc1p, sh, gate, w_router, w_gate, w_up, w_down, blk, final_g=None):
    bn, l, d = x.shape
    t = bn * l
    tok, info, cnt = _moe_route(x, gain, sc1p, sh, w_router)
    counts = cnt[0, :N_EXPERTS].astype(jnp.int32)
    padded = (counts + blk - 1) // blk * blk
    pad_end = jnp.cumsum(padded)
    pad_start = pad_end - padded
    n_blocks = -(-(2 * t) // blk) + N_EXPERTS
    block_e = jnp.minimum(
        jnp.sum((jnp.arange(n_blocks, dtype=jnp.int32)[:, None] * blk >= pad_end[None, :]).astype(jnp.int32), axis=1),
        N_EXPERTS - 1).astype(jnp.int32)
    info2 = info.reshape(t, LANES)
    ee = jnp.arange(N_EXPERTS, dtype=jnp.int32)[None, :]

    def slot(e_col, r_col):
        e = info2[:, e_col].astype(jnp.int32)
        return jnp.sum(jnp.where(e[:, None] == ee, pad_start[None, :], 0), axis=1) + info2[:, r_col].astype(jnp.int32)

    tb = min(256, l)
    dest = jnp.stack([slot(0, 2), slot(1, 3)], axis=0).reshape(2, t // tb, tb).transpose(1, 0, 2)
    xb = _moe_dispatch(tok.reshape(t, d), dest, n_blocks * blk)
    hmid = _moe_up(xb, block_e, w_gate, w_up, blk)
    yb = _moe_down(hmid, block_e, w_down, blk)
    return _moe_combine(yb, dest, info, x, gate, final_g)


def kernel(x, c, ctx, c_ctx, ada_w, ada_b, norm1_g, norm2_g, normf_g, cf_w_pw1, cf_b_pw1, cf_w_dw, cf_b_dw, cf_ln_g, cf_ln_b, cf_w_pw2, cf_b_pw2, hy_w_in, hy_b_in, hy_w_short, hy_b_short, hy_w_f1, hy_b_f1, hy_freq, hy_w_f2, hy_b_f2, hy_w_f3, hy_bias, hy_w_out, hy_b_out, hg_lb_logits, hg_w_in, hg_gn_g, hg_w_out, ffn_w_gate, ffn_w_up, ffn_w_down, moe_w_router, moe_w_gate, moe_w_up, moe_w_down):
    bn, s, d = x.shape
    cl = ctx.shape[1]
    depth = ada_w.shape[0]
    bf = lambda a: a.astype(BF16)

    cvecs = jnp.concatenate([c, c_ctx[None], jnp.zeros((8 - bn - 1, d), F32)], axis=0)
    mods = _ada_mod(cvecs, ada_w, ada_b)

    def mod_slices(i, ctx_rows):
        m = jnp.broadcast_to(mods[i, bn:bn + 1], (bn, N_MOD * d)) if ctx_rows else mods[i, :bn]
        parts = [m[:, k * d:(k + 1) * d].reshape(bn, 1, d) for k in range(N_MOD)]
        sh1, sc1, g1, sh2, sc2, g2 = parts
        return sh1, 1.0 + sc1, g1, sh2, 1.0 + sc2, g2

    p = jax.nn.softmax(hg_lb_logits.astype(F32), axis=0)
    lb_all = jnp.cumsum(p, axis=0) - p[0]

    x = _embed(x)
    xc = ctx

    for i in range(depth):
        last = i == depth - 1
        kind, j = i % 3, i // 3
        streams = [(x, mod_slices(i, False), False)]
        if not last:
            streams.append((xc, mod_slices(i, True), True))

        new = []
        if kind == 0:
            wa, wg = bf(cf_w_pw1[j][:, :d]), bf(cf_w_pw1[j][:, d:])
            ba, bg = cf_b_pw1[j][:d], cf_b_pw1[j][d:]
            w2 = bf(cf_w_pw2[j])
            for xs, (sh1, sc1, g1, _, _, _), _ in streams:
                u = _nmm(xs, norm1_g[i], sc1, sh1, [wa, wg], [ba, bg], epi="glu")
                new.append(_conformer_tail(u, cf_w_dw[j], cf_b_dw[j], cf_ln_g[j], cf_ln_b[j], w2, cf_b_pw2[j], xs, g1))
        elif kind == 1:
            order = hy_bias.shape[1]
            w_in_t = bf(hy_w_in[j].T)
            w_out = bf(hy_w_out[j])
            for xs, (sh1, sc1, g1, _, _, _), _ in streams:
                l = xs.shape[1]
                n2, r, nk, lp = _fft_plan(l)
                consts = _fft_consts(r, nk, n2)
                filt = _hyena_filters_t(l, lp, hy_w_f1[j], hy_b_f1[j], hy_freq[j], hy_w_f2[j], hy_b_f2[j],
                                        hy_w_f3[j], order, d)
                spec = _hyena_spectrum(filt, consts, r, nk, n2)
                ut = _nmm(xs, norm1_g[i], sc1, sh1, [w_in_t], [hy_b_in[j]], transposed=True)
                if lp > l:
                    ut = jnp.pad(ut, ((0, 0), (0, 0), (0, lp - l)))
                zt = _hyena_conv(ut, hy_w_short[j], hy_b_short[j], hy_bias[j], spec, consts, r, nk, n2, l)
                new.append(_mmres(zt, w_out, hy_b_out[j], xs, g1, trans_a=True))
        else:
            w_in = bf(hg_w_in[j])
            w_out = bf(hg_w_out[j])
            nh = d // HG_HEAD
            state = jnp.zeros((bn, 2, nh, HG_HEAD, HG_HEAD), F32)
            outs = {}
            for xs, (sh1, sc1, g1, _, _, _), is_ctx in ((xc, mod_slices(i, True), True), streams[0]):
                u5 = _nmm(xs, norm1_g[i], sc1, sh1, [w_in], [jnp.zeros((5 * d,), F32)])
                o_f, o_b, state = _hgrn_scan(u5, lb_all[i], state)
                if not (is_ctx and last):
                    outs[is_ctx] = _hgrn_out(o_f, o_b, u5, hg_gn_g[j], w_out, xs, g1)
            new = [outs[False]] + ([outs[True]] if not last else [])

        x = new[0]
        if not last:
            xc = new[1]
        streams = [(x, mod_slices(i, False), False)]
        if not last:
            streams.append((xc, mod_slices(i, True), True))

        new = []
        if i % 2 == 0:
            wg, wu, wd = bf(ffn_w_gate[i // 2]), bf(ffn_w_up[i // 2]), bf(ffn_w_down[i // 2])
            zb = jnp.zeros((wg.shape[1],), F32)
            for xs, (_, _, _, sh2, sc2, g2), _ in streams:
                hmid = _nmm(xs, norm2_g[i], sc2, sh2, [wg, wu], [zb, zb], epi="swiglu", out_dtype=BF16)
                new.append(_mmres(hmid, wd, jnp.zeros((d,), F32), xs, g2))
        else:
            wg, wu, wd = bf(moe_w_gate[i // 2]), bf(moe_w_up[i // 2]), bf(moe_w_down[i // 2])
            for xs, (_, _, _, sh2, sc2, g2), is_ctx in streams:
                new.append(_moe_layer(xs, norm2_g[i], sc2, sh2, g2, moe_w_router[i // 2], wg, wu, wd,
                                      blk=128 if is_ctx else 512, final_g=normf_g if last else None))
        x = new[0]
        if not last:
            xc = new[1]

    return x if (depth - 1) % 2 == 1 else _final_norm(x, normf_g)


def _fnorm_kernel(x_ref, g_ref, o_ref):
    xv = x_ref[0]
    o_ref[0] = xv * lax.rsqrt(jnp.mean(xv * xv, axis=-1, keepdims=True) + EPS) * g_ref[...]


def _final_norm(x, g):
    bn, l, d = x.shape
    tm = min(512, l)
    return pl.pallas_call(
        _fnorm_kernel,
        grid=(bn, l // tm),
        in_specs=[pl.BlockSpec((1, tm, d), lambda b, i: (b, i, 0)), pl.BlockSpec((1, d), lambda b, i: (0, 0))],
        out_specs=pl.BlockSpec((1, tm, d), lambda b, i: (b, i, 0)),
        out_shape=jax.ShapeDtypeStruct(x.shape, F32),
        compiler_params=_cp(("parallel", "parallel")),
        name="final_norm",
    )(x, g.reshape(1, d))
```

```python
import functools
import math

import numpy as np
import jax
import jax.numpy as jnp
from jax import lax
from jax.experimental import pallas as pl
from jax.experimental.pallas import tpu as pltpu

F32 = jnp.float32
BF16 = jnp.bfloat16
EPS = 1e-6

GRID_W = 64
N_MOD = 6
CONV_WIDTH = 31
CONV_HALO = 16
HY_EMB = 33
HY_EMB_PAD = 40
HY_TARGET = 1e-2
HY_DECAY_MIN = math.log(HY_TARGET) / 1.5
HY_DECAY_MAX = math.log(HY_TARGET) / 0.3
HG_HEAD = 128
HG_CHUNK = 64
N_EXPERTS = 8
LANES = 128
DMA_UNROLL = 8
VMEM_LIMIT = 56 * 1024 * 1024


def _cp(sem, vmem=VMEM_LIMIT):
    return pltpu.CompilerParams(dimension_semantics=sem, vmem_limit_bytes=vmem)


def _dot(a, b):
    return jnp.dot(a, b, preferred_element_type=F32)


def _dot_nt(a, b):
    return lax.dot_general(a, b, (((1,), (1,)), ((), ())), preferred_element_type=F32)


def _dot_tn(a, b):
    return lax.dot_general(a, b, (((0,), (0,)), ((), ())), preferred_element_type=F32)


def _split3(a):
    h = a.astype(BF16)
    r = a - h.astype(F32)
    m = r.astype(BF16)
    l = (r - m.astype(F32)).astype(BF16)
    return h, m, l


def _dot_acc(a, b, dot=_dot):
    ah, am, _ = _split3(a)
    bh, bm, _ = _split3(b)
    return dot(ah, bh) + (dot(ah, bm) + dot(am, bh))


def _silu(x):
    return x * jax.nn.sigmoid(x)


def _ada_kernel(c_ref, w_ref, b_ref, o_ref):
    s = _silu(c_ref[...])
    o_ref[0] = _dot_acc(s, w_ref[0]) + b_ref[0]


def _ada_mod(cvecs, ada_w, ada_b):
    depth, d, n = ada_w.shape
    tn = 1024
    return pl.pallas_call(
        _ada_kernel,
        grid=(depth, n // tn),
        in_specs=[pl.BlockSpec((8, d), lambda i, j: (0, 0)),
                  pl.BlockSpec((1, d, tn), lambda i, j: (i, 0, j)),
                  pl.BlockSpec((1, 1, tn), lambda i, j: (i, 0, j))],
        out_specs=pl.BlockSpec((1, 8, tn), lambda i, j: (i, 0, j)),
        out_shape=jax.ShapeDtypeStruct((depth, 8, n), F32),
        compiler_params=_cp(("parallel", "parallel")),
        name="ada_mod",
    )(cvecs, ada_w, ada_b.reshape(depth, 1, n))


def _embed_kernel(x_ref, om_ref, o_ref, pos_ref, *, tm):
    i = pl.program_id(0)

    @pl.when(pl.program_id(1) == 0)
    def _():
        q = om_ref.shape[1]
        om = om_ref[...]
        b = lax.broadcasted_iota(jnp.int32, (GRID_W, 1), 0).astype(F32) * om
        sin_c, cos_c = jnp.sin(b), jnp.cos(b)
        for g in range(tm // GRID_W):
            rows = slice(g * GRID_W, (g + 1) * GRID_W)
            a = (i * (tm // GRID_W) + g).astype(F32) * om
            pos_ref[rows, 0 * q:1 * q] = jnp.broadcast_to(jnp.sin(a), (GRID_W, q))
            pos_ref[rows, 1 * q:2 * q] = jnp.broadcast_to(jnp.cos(a), (GRID_W, q))
            pos_ref[rows, 2 * q:3 * q] = sin_c
            pos_ref[rows, 3 * q:4 * q] = cos_c

    o_ref[0] = x_ref[0] + pos_ref[...]


def _embed(x):
    bn, s, d = x.shape
    tm = min(512, s)
    assert tm % GRID_W == 0 and s % tm == 0
    q = d // 4
    omega = (1.0 / (10000.0 ** (jnp.arange(q, dtype=F32) / q))).reshape(1, q)
    return pl.pallas_call(
        functools.partial(_embed_kernel, tm=tm),
        grid=(s // tm, bn),
        in_specs=[pl.BlockSpec((1, tm, d), lambda i, b: (b, i, 0)),
                  pl.BlockSpec((1, q), lambda i, b: (0, 0))],
        out_specs=pl.BlockSpec((1, tm, d), lambda i, b: (b, i, 0)),
        out_shape=jax.ShapeDtypeStruct(x.shape, F32),
        scratch_shapes=[pltpu.VMEM((tm, d), F32)],
        compiler_params=_cp(("parallel", "arbitrary")),
        name="pos_embed",
    )(x, omega)


def _norm_mod(x, g, sc1p, sh):
    ms = jnp.mean(x * x, axis=-1, keepdims=True)
    return (x * lax.rsqrt(ms + EPS) * g) * sc1p + sh


def _nmm_kernel(*refs, nw, epi, transposed):
    x_ref, g_ref, sc_ref, sh_ref = refs[:4]
    w_refs = refs[4:4 + nw]
    b_refs = refs[4 + nw:4 + 2 * nw]
    o_ref = refs[4 + 2 * nw]
    hs_ref = refs[5 + 2 * nw]

    @pl.when(pl.program_id(2) == 0)
    def _():
        hs_ref[...] = _norm_mod(x_ref[0], g_ref[...], sc_ref[0], sh_ref[0]).astype(BF16)

    h = hs_ref[...]
    if transposed:
        r = _dot_nt(w_refs[0][...], h) + b_refs[0][...]
    else:
        outs = [_dot(h, w[...]) + b[...] for w, b in zip(w_refs, b_refs)]
        if epi == "plain":
            r = outs[0]
        elif epi == "glu":
            r = outs[0] * jax.nn.sigmoid(outs[1])
        else:
            r = _silu(outs[0]) * outs[1]
    o_ref[0] = r.astype(o_ref.dtype)


def _col_tile(n, target=1024):
    best = None
    for t in range(256, n + 1, 256):
        if n % t == 0 and t <= target * 7 // 4:
            best = t
    return best if best is not None else n


def _nmm(x, gain, sc1p, sh, ws, bs, *, epi="plain", out_dtype=F32, transposed=False):
    bn, l, d = x.shape
    n = ws[0].shape[0] if transposed else ws[0].shape[1]
    tn = _col_tile(n)
    tm = min(1024 if len(ws) * tn <= 2048 else 512, l)
    nw = len(ws)
    if transposed:
        w_specs = [pl.BlockSpec((tn, d), lambda b, i, j: (j, 0))]
        b_specs = [pl.BlockSpec((tn, 1), lambda b, i, j: (j, 0))]
        bs = [bs[0].reshape(n, 1)]
        out_spec = pl.BlockSpec((1, tn, tm), lambda b, i, j: (b, j, i))
        out_shape = jax.ShapeDtypeStruct((bn, n, l), out_dtype)
    else:
        w_specs = [pl.BlockSpec((d, tn), lambda b, i, j: (0, j)) for _ in ws]
        b_specs = [pl.BlockSpec((1, tn), lambda b, i, j: (0, j)) for _ in ws]
        bs = [b.reshape(1, n) for b in bs]
        out_spec = pl.BlockSpec((1, tm, tn), lambda b, i, j: (b, i, j))
        out_shape = jax.ShapeDtypeStruct((bn, l, n), out_dtype)
    vec = pl.BlockSpec((1, 1, d), lambda b, i, j: (b, 0, 0))
    return pl.pallas_call(
        functools.partial(_nmm_kernel, nw=nw, epi=epi, transposed=transposed),
        grid=(bn, l // tm, n // tn),
        in_specs=[pl.BlockSpec((1, tm, d), lambda b, i, j: (b, i, 0)),
                  pl.BlockSpec((1, d), lambda b, i, j: (0, 0)), vec, vec] + w_specs + b_specs,
        out_specs=out_spec,
        out_shape=out_shape,
        scratch_shapes=[pltpu.VMEM((tm, d), BF16)],
        compiler_params=_cp(("parallel", "parallel", "arbitrary")),
        name="norm_mod_matmul_" + ("t" if transposed else epi),
    )(x, gain.reshape(1, d), sc1p, sh, *ws, *bs)


def _mmres_kernel(a_ref, w_ref, b_ref, x_ref, g_ref, o_ref, *, trans_a):
    a = a_ref[0]
    y = (_dot_tn(a, w_ref[...]) if trans_a else _dot(a, w_ref[...])) + b_ref[...]
    o_ref[0] = x_ref[0] + g_ref[0] * y


def _mmres(a, w, bias, x, gate, *, trans_a=False):
    bn, l, n = x.shape
    k = w.shape[0]
    tm = min(512, l)
    a_spec = (pl.BlockSpec((1, k, tm), lambda b, i: (b, 0, i)) if trans_a
              else pl.BlockSpec((1, tm, k), lambda b, i: (b, i, 0)))
    return pl.pallas_call(
        functools.partial(_mmres_kernel, trans_a=trans_a),
        grid=(bn, l // tm),
        in_specs=[a_spec,
                  pl.BlockSpec((k, n), lambda b, i: (0, 0)),
                  pl.BlockSpec((1, n), lambda b, i: (0, 0)),
                  pl.BlockSpec((1, tm, n), lambda b, i: (b, i, 0)),
                  pl.BlockSpec((1, 1, n), lambda b, i: (b, 0, 0))],
        out_specs=pl.BlockSpec((1, tm, n), lambda b, i: (b, i, 0)),
        out_shape=jax.ShapeDtypeStruct(x.shape, F32),
        input_output_aliases={3: 0},
        compiler_params=_cp(("parallel", "parallel")),
        name="matmul_residual" + ("_ta" if trans_a else ""),
    )(a, w, bias.reshape(1, n), x, gate)


def _cf2_kernel(prev_ref, cur_ref, next_ref, wdw_ref, bdw_ref, lng_ref, lnb_ref, w2_ref, b2_ref,
                x_ref, g_ref, o_ref, buf_ref, sh_ref, cv_ref, *, tm):
    i = pl.program_id(1)
    nt = pl.num_programs(1)
    h = CONV_HALO
    buf_ref[0:h] = jnp.where(i > 0, prev_ref[0], 0.0)
    buf_ref[h:h + tm] = cur_ref[0]
    buf_ref[h + tm:2 * h + tm] = jnp.where(i < nt - 1, next_ref[0], 0.0)
    span = sh_ref.shape[1]
    for s in range(8):
        sh_ref[s] = buf_ref[s:s + span]
    off = h - (CONV_WIDTH - 1) // 2
    rc = 8
    for c in range(tm // rc):
        acc = None
        for k in range(CONV_WIDTH):
            q, s = divmod(off + k, 8)
            term = sh_ref[s, c * rc + 8 * q:c * rc + 8 * q + rc] * wdw_ref[k]
            acc = term if acc is None else acc + term
        cv_ref[c * rc:(c + 1) * rc] = acc + bdw_ref[...]
    u = cv_ref[...]
    mu = jnp.mean(u, axis=-1, keepdims=True)
    uc = u - mu
    var = jnp.mean(uc * uc, axis=-1, keepdims=True)
    v = _silu(uc * lax.rsqrt(var + EPS) * lng_ref[...] + lnb_ref[...])
    y = _dot(v.astype(BF16), w2_ref[...]) + b2_ref[...]
    o_ref[0] = x_ref[0] + g_ref[0] * y


def _conformer_tail(u, w_dw, b_dw, ln_g, ln_b, w2, b2, x, gate):
    bn, l, d = x.shape
    tm = min(256, l)
    h = CONV_HALO
    nh = l // h
    r = tm // h
    row = lambda a: a.reshape(1, d)
    full = lambda shp: pl.BlockSpec(shp, lambda b, i: (0,) * len(shp))
    return pl.pallas_call(
        functools.partial(_cf2_kernel, tm=tm),
        grid=(bn, l // tm),
        in_specs=[pl.BlockSpec((1, h, d), lambda b, i: (b, jnp.maximum(i * r - 1, 0), 0)),
                  pl.BlockSpec((1, tm, d), lambda b, i: (b, i, 0)),
                  pl.BlockSpec((1, h, d), lambda b, i: (b, jnp.minimum((i + 1) * r, nh - 1), 0)),
                  full((CONV_WIDTH, 8, d)), full((1, d)), full((1, d)), full((1, d)),
                  full((d, d)), full((1, d)),
                  pl.BlockSpec((1, tm, d), lambda b, i: (b, i, 0)),
                  pl.BlockSpec((1, 1, d), lambda b, i: (b, 0, 0))],
        out_specs=pl.BlockSpec((1, tm, d), lambda b, i: (b, i, 0)),
        out_shape=jax.ShapeDtypeStruct(x.shape, F32),
        scratch_shapes=[pltpu.VMEM((tm + 2 * h, d), F32),
                        pltpu.VMEM((8, tm + 2 * h - 8, d), F32),
                        pltpu.VMEM((tm, d), F32)],
        input_output_aliases={9: 0},
        compiler_params=_cp(("parallel", "parallel")),
        name="conformer_tail",
    )(u, u, u, jnp.broadcast_to(w_dw[:, None, :], (CONV_WIDTH, 8, d)), row(b_dw), row(ln_g), row(ln_b),
      w2, row(b2), x, gate)


def _fft_plan(l):
    n2 = 256 if l >= 4096 else 128
    r = max(16, -(-l // n2))
    nk = -(-(r + 1) // 16) * 16
    return n2, r, nk, r * n2


def _fft_channels(r, n2, d):
    return max(8, min(64, d, (128 * 1024) // (r * n2)))


def _fft_consts(r, nk, n2):
    n1 = 2 * r
    n = n1 * n2
    k1 = np.arange(nk, dtype=np.float64)[:, None]
    live = (k1 <= r).astype(np.float64)
    a1 = 2.0 * np.pi * k1 * np.arange(r, dtype=np.float64)[None, :] / n1
    fc1 = np.concatenate([np.cos(a1) * live, -np.sin(a1) * live], axis=0)
    at = 2.0 * np.pi * k1 * np.arange(n2, dtype=np.float64)[None, :] / n
    tw = np.stack([np.cos(at), -np.sin(at)], axis=0)
    a2 = 2.0 * np.pi * np.outer(np.arange(n2, dtype=np.float64), np.arange(n2, dtype=np.float64)) / n2
    f2 = np.concatenate([np.cos(a2), -np.sin(a2)], axis=1)
    g2 = np.concatenate([np.cos(a2), np.sin(a2)], axis=1)
    wk = (live * np.where((k1 == 0) | (k1 == r), 1.0, 2.0)).T
    ai = 2.0 * np.pi * np.arange(r, dtype=np.float64)[:, None] * k1.T / n1
    c1i = np.concatenate([np.cos(ai) * wk, -np.sin(ai) * wk], axis=1) / n
    return (jnp.asarray(fc1, BF16), jnp.asarray(tw, F32), jnp.asarray(f2, BF16),
            jnp.asarray(g2, BF16), jnp.asarray(c1i, BF16))


def _fft_fwd_rows(z, fc1, tr, ti):
    n1 = tr.shape[0]
    a = _dot(fc1, z.astype(BF16))
    ar, ai = a[:n1], a[n1:]
    return ar * tr - ai * ti, ar * ti + ai * tr


def _cplx_rows(p, ch, n1, n2):
    p4 = p.reshape(ch, 2, n1, 2 * n2)
    return p4[:, 0, :, :n2] - p4[:, 1, :, n2:], p4[:, 0, :, n2:] + p4[:, 1, :, :n2]


def _hyfilt_kernel(w1_ref, b1_ref, fr_ref, w2_ref, b2_ref, fb_ref, w3_ref, dl_ref, o_ref, hid_ref, *, l, lp):
    lane = lax.broadcasted_iota(jnp.int32, (1, lp), 1)
    pos = lane.astype(F32)
    t = pos / (l - 1)

    @pl.when((pl.program_id(0) == 0) & (pl.program_id(1) == 0))
    def _():
        w = (2.0 * math.pi) * pos / l
        arg = fb_ref[...] * w
        row = lax.broadcasted_iota(jnp.int32, (HY_EMB_PAD, 1), 0)
        bands = (HY_EMB - 1) // 2
        z = jnp.where(row == 0, t,
                      jnp.where(row <= bands, jnp.cos(arg),
                                jnp.where(row < HY_EMB, -jnp.sin(arg), 0.0)))
        fr = fr_ref[...]
        hid = jnp.sin(fr * (_dot_acc(w1_ref[...], z) + b1_ref[...]))
        for j in range(w2_ref.shape[0]):
            hid = jnp.sin(fr * (_dot_acc(w2_ref[j], hid) + b2_ref[j]))
        hh, hm, _ = _split3(hid)
        hid_ref[0] = hh
        hid_ref[1] = hm

    decay = jnp.exp(-t * dl_ref[...])
    hh, hm = hid_ref[0], hid_ref[1]
    valid = lane < l

    def taps(w):
        wh, wm, _ = _split3(w)
        return _dot(wh, hh) + (_dot(wh, hm) + _dot(wm, hh))

    f0 = jnp.where(valid, taps(w3_ref[0, 0]) * decay, 0.0)
    f1 = jnp.where(valid & (lane > 0), taps(w3_ref[0, 1]) * decay, 0.0)
    norm = jnp.sum(jnp.abs(f0), axis=-1, keepdims=True) + jnp.sum(jnp.abs(f1), axis=-1, keepdims=True)
    o_ref[0, 0] = f0 / norm
    o_ref[0, 1] = f1 / norm


def _hyena_filters_t(l, lp, w_f1, b_f1, freq, w_f2, b_f2, w_f3, order, d):
    fw = w_f1.shape[1]
    bands = (HY_EMB - 1) // 2
    fb = jnp.linspace(1e-4, bands - 1, bands, dtype=F32)
    fbcol = jnp.concatenate([jnp.zeros((1,), F32), fb, fb, jnp.zeros((HY_EMB_PAD - HY_EMB,), F32)]).reshape(-1, 1)
    w1t = jnp.pad(w_f1.T, ((0, 0), (0, HY_EMB_PAD - HY_EMB)))
    w2t = jnp.swapaxes(w_f2, 1, 2)
    w3t = w_f3.T.reshape(order, 2, d, fw)
    deltas = jnp.abs(jnp.linspace(HY_DECAY_MIN, HY_DECAY_MAX, d, dtype=F32)).reshape(d, 1)
    tr = 32
    c2 = lambda shp: pl.BlockSpec(shp, lambda o, c: (0,) * len(shp))
    return pl.pallas_call(
        functools.partial(_hyfilt_kernel, l=l, lp=lp),
        grid=(order, d // tr),
        in_specs=[c2((fw, HY_EMB_PAD)), c2((fw, 1)), c2((fw, 1)), c2(w2t.shape), c2((w_f2.shape[0], fw, 1)),
                  c2((HY_EMB_PAD, 1)),
                  pl.BlockSpec((1, 2, tr, fw), lambda o, c: (o, 0, c, 0)),
                  pl.BlockSpec((tr, 1), lambda o, c: (c, 0))],
        out_specs=pl.BlockSpec((1, 2, tr, lp), lambda o, c: (o, 0, c, 0)),
        out_shape=jax.ShapeDtypeStruct((order, 2, d, lp), F32),
        scratch_shapes=[pltpu.VMEM((2, fw, lp), BF16)],
        compiler_params=_cp(("arbitrary", "arbitrary")),
        name="hyena_filter",
    )(w1t, b_f1.reshape(fw, 1), freq.reshape(fw, 1), w2t, b_f2.reshape(-1, fw, 1), fbcol, w3t, deltas)


def _hyspec_kernel(f_ref, fc1_ref, tw_ref, f2_ref, o_ref, ys_ref, *, dc, n1, n2):
    tr, ti = tw_ref[0], tw_ref[1]
    fc1 = fc1_ref[...]

    def stage1(c, _):
        for s in range(2):
            yr, yi = _fft_fwd_rows(f_ref[0, s, c], fc1, tr, ti)
            ys_ref[s, c, :n1] = yr.astype(BF16)
            ys_ref[s, c, n1:] = yi.astype(BF16)
        return 0

    for c in range(dc):
        stage1(c, 0)
    zs = []
    for s in range(2):
        p = _dot(ys_ref[s].reshape(dc * 2 * n1, n2), f2_ref[...])
        zs.append(_cplx_rows(p, dc, n1, n2))
    o_ref[0, 0] = zs[0][0] + zs[1][0]
    o_ref[0, 1] = zs[0][1] - zs[1][1]


def _hyena_spectrum(filt, consts, r, n1, n2):
    order, _, d, lp = filt.shape
    dc = _fft_channels(r, n2, d)
    fc1, tw, f2, _, _ = consts
    c2 = lambda a: pl.BlockSpec(a.shape, lambda o, c: (0,) * a.ndim)
    return pl.pallas_call(
        functools.partial(_hyspec_kernel, dc=dc, n1=n1, n2=n2),
        grid=(order, d // dc),
        in_specs=[pl.BlockSpec((1, 2, dc, r, n2), lambda o, c: (o, 0, c, 0, 0)), c2(fc1), c2(tw), c2(f2)],
        out_specs=pl.BlockSpec((1, 2, dc, n1, n2), lambda o, c: (o, 0, c, 0, 0)),
        out_shape=jax.ShapeDtypeStruct((order, 2, d, n1, n2), F32),
        scratch_shapes=[pltpu.VMEM((2, dc, 2 * n1, n2), BF16)],
        compiler_params=_cp(("parallel", "parallel")),
        name="hyena_spectrum",
    )(filt.reshape(order, 2, d, r, n2), fc1, tw, f2)


def _hyconv_kernel(wsh_ref, hb_ref, v_ref, x1_ref, x2_ref, h_ref, fc1_ref, tw_ref, f2_ref, g2_ref, c1i_ref,
                   o_ref, cs_ref, ys_ref, vs_ref, *, dc, ch, r, n1, n2, d, l, order):
    cb = pl.program_id(0)
    tr, ti = tw_ref[0], tw_ref[1]
    row = lax.broadcasted_iota(jnp.int32, (r, n2), 0)
    lane = lax.broadcasted_iota(jnp.int32, (r, n2), 1)
    valid = (row * n2 + lane) < l

    def short_conv(c, _):
        chan = cb * dc + c
        for s, ref in enumerate((v_ref, x1_ref, x2_ref)):
            u = ref[0, 0, c]
            idx = s * d + chan
            a = pltpu.roll(u, 1, axis=1)
            prev = jnp.where(lane == 0, jnp.where(row == 0, 0.0, pltpu.roll(a, 1, axis=0)), a)
            b = pltpu.roll(u, n2 - 1, axis=1)
            nxt = jnp.where(lane == n2 - 1, jnp.where(row == r - 1, 0.0, pltpu.roll(b, r - 1, axis=0)), b)
            y = wsh_ref[0, idx] * prev + wsh_ref[1, idx] * u + wsh_ref[2, idx] * nxt + wsh_ref[3, idx]
            cs_ref[s, c] = jnp.where(valid, y, 0.0)
        return 0

    for c in range(dc):
        short_conv(c, 0)

    for n in range(order):
        def stage1(c, _):
            yr, yi = _fft_fwd_rows(cs_ref[0, c], fc1_ref[...], tr, ti)
            ys_ref[c, :n1] = yr.astype(BF16)
            ys_ref[c, n1:] = yi.astype(BF16)
            return 0

        for c in range(dc):
            stage1(c, 0)

        for c0 in range(0, dc, ch):
            p = _dot(ys_ref[c0:c0 + ch].reshape(ch * 2 * n1, n2), f2_ref[...])
            zr, zi = _cplx_rows(p, ch, n1, n2)
            hr, hi = h_ref[n, 0, c0:c0 + ch], h_ref[n, 1, c0:c0 + ch]
            vs_ref[c0:c0 + ch, :n1] = (zr * hr - zi * hi).astype(BF16)
            vs_ref[c0:c0 + ch, n1:] = (zr * hi + zi * hr).astype(BF16)
            q = _dot(vs_ref[c0:c0 + ch].reshape(ch * 2 * n1, n2), g2_ref[...])
            ur, ui = _cplx_rows(q, ch, n1, n2)
            ys_ref[c0:c0 + ch, :n1] = (ur * tr + ui * ti).astype(BF16)
            ys_ref[c0:c0 + ch, n1:] = (ui * tr - ur * ti).astype(BF16)

        def stage4(c, _):
            chan = cb * dc + c
            conv = _dot(c1i_ref[...], ys_ref[c])
            z = cs_ref[n + 1, c] * (conv + hb_ref[n, chan] * cs_ref[0, c])
            if n == order - 1:
                o_ref[0, c] = z.astype(o_ref.dtype)
            else:
                cs_ref[0, c] = z
            return 0

        for c in range(dc):
            stage4(c, 0)


def _hyena_conv(ut, w_short, b_short, hy_bias, spec, consts, r, n1, n2, l):
    bn, d3, lp = ut.shape
    d = d3 // 3
    order = hy_bias.shape[0]
    dc = _fft_channels(r, n2, d)
    ch = max(c for c in range(1, dc + 1) if dc % c == 0 and (c == 1 or c * 2 * n1 <= 1536))
    fc1, tw, f2, g2, c1i = consts
    wsh = jnp.concatenate([w_short, b_short.reshape(1, d3)], axis=0)
    u5 = ut.reshape(bn, 3, d, r, n2)
    cst = lambda a: pl.BlockSpec(a.shape, lambda c, b, *_: (0,) * a.ndim)
    stream = lambda s: pl.BlockSpec((1, 1, dc, r, n2), lambda c, b, *_: (b, s, c, 0, 0))
    grid_spec = pltpu.PrefetchScalarGridSpec(
        num_scalar_prefetch=2,
        grid=(d // dc, bn),
        in_specs=[stream(0), stream(1), stream(2),
                  pl.BlockSpec((order, 2, dc, n1, n2), lambda c, b, *_: (0, 0, c, 0, 0)),
                  cst(fc1), cst(tw), cst(f2), cst(g2), cst(c1i)],
        out_specs=pl.BlockSpec((1, dc, r, n2), lambda c, b, *_: (b, c, 0, 0)),
        scratch_shapes=[pltpu.VMEM((3, dc, r, n2), F32),
                        pltpu.VMEM((dc, 2 * n1, n2), BF16),
                        pltpu.VMEM((dc, 2 * n1, n2), BF16)],
    )
    out = pl.pallas_call(
        functools.partial(_hyconv_kernel, dc=dc, ch=ch, r=r, n1=n1, n2=n2, d=d, l=l, order=order),
        grid_spec=grid_spec,
        out_shape=jax.ShapeDtypeStruct((bn, d, r, n2), BF16),
        compiler_params=_cp(("parallel", "arbitrary")),
        name="hyena_conv",
    )(wsh, hy_bias, u5, u5, u5, spec, fc1, tw, f2, g2, c1i)
    return out.reshape(bn, d, lp)


def _scan_kernel(qf_ref, vf_ref, zf_ref, qb_ref, vb_ref, zb_ref, lb_ref, tri_ref, s0_ref,
                 of_ref, ob_ref, sf_ref, st_ref, *, tt, hp):
    i = pl.program_id(2)
    ck = HG_CHUNK
    nck = tt // ck

    @pl.when(i == 0)
    def _():
        st_ref[...] = s0_ref[0]

    for h in range(hp):
        hs = slice(h * HG_HEAD, (h + 1) * HG_HEAD)
        lb = lb_ref[:, hs]
        for di, (q_ref, v_ref, z_ref, o_ref) in enumerate(((qf_ref, vf_ref, zf_ref, of_ref),
                                                           (qb_ref, vb_ref, zb_ref, ob_ref))):
            tri = tri_ref[di]
            f = lb + (1.0 - lb) * jax.nn.sigmoid(z_ref[0, :, hs])
            g = jnp.log(f)
            k = 1.0 - f
            gh, gm, _ = _split3(g)
            b = _dot(tri, gh) + _dot(tri, gm)
            tot = [b[c * ck + ck - 1:c * ck + ck] if di == 0 else b[c * ck:c * ck + 1] for c in range(nck)]
            btot = jnp.concatenate([jnp.broadcast_to(r, (ck, HG_HEAD)) for r in tot], axis=0)
            qd = (_silu(q_ref[0, :, hs]) * jnp.exp(b)).astype(BF16)
            ki = (k * jnp.exp(-b)).astype(BF16)
            ke = (k * jnp.exp(btot - b)).astype(BF16)
            vb = v_ref[0, :, hs].astype(BF16)
            att = jnp.where(tri != 0, _dot_nt(qd, ki), 0.0)
            o_intra = _dot(att.astype(BF16), vb)
            kv = [_dot_tn(vb[c * ck:(c + 1) * ck], ke[c * ck:(c + 1) * ck]) for c in range(nck)]
            st = st_ref[di, h]
            inter = [None] * nck
            for c in (range(nck) if di == 0 else reversed(range(nck))):
                inter[c] = _dot_nt(qd[c * ck:(c + 1) * ck], st.astype(BF16))
                st = st * jnp.exp(tot[c]) + kv[c]
            st_ref[di, h] = st
            o_ref[0, :, hs] = o_intra + jnp.concatenate(inter, axis=0)

    @pl.when(i == pl.num_programs(2) - 1)
    def _():
        sf_ref[0] = st_ref[...]


def _hgrn_scan(u5, lb, s0):
    bn, l, d5 = u5.shape
    d = d5 // 5
    nh = d // HG_HEAD
    hp = 4 if nh % 4 == 0 else 1
    hw = hp * HG_HEAD
    ng = nh // hp
    tt = min(256, l)
    nt = l // tt
    col = lambda s, rev: pl.BlockSpec(
        (1, tt, hw), (lambda b, h, i: (b, nt - 1 - i, s * ng + h)) if rev else (lambda b, h, i: (b, i, s * ng + h)))
    st_spec = pl.BlockSpec((1, 2, hp, HG_HEAD, HG_HEAD), lambda b, h, i: (b, 0, h, 0, 0))
    t_idx = np.arange(tt)
    same = (t_idx[:, None] // HG_CHUNK) == (t_idx[None, :] // HG_CHUNK)
    tri = jnp.asarray(np.stack([same & (t_idx[:, None] >= t_idx[None, :]),
                                same & (t_idx[:, None] <= t_idx[None, :])]), BF16)
    return pl.pallas_call(
        functools.partial(_scan_kernel, tt=tt, hp=hp),
        grid=(bn, ng, nt),
        in_specs=[col(0, False), col(1, False), col(3, False), col(0, True), col(1, True), col(4, True),
                  pl.BlockSpec((1, hw), lambda b, h, i: (0, h)),
                  pl.BlockSpec((2, tt, tt), lambda b, h, i: (0, 0, 0)), st_spec],
        out_specs=[col(0, False), col(0, True), st_spec],
        out_shape=[jax.ShapeDtypeStruct((bn, l, d), F32), jax.ShapeDtypeStruct((bn, l, d), F32),
                   jax.ShapeDtypeStruct(s0.shape, F32)],
        scratch_shapes=[pltpu.VMEM((2, hp, HG_HEAD, HG_HEAD), F32)],
        compiler_params=_cp(("parallel", "parallel", "arbitrary")),
        name="hgrn2_scan",
    )(u5, u5, u5, u5, u5, u5, lb.reshape(1, d), tri, s0)


def _hgout_kernel(of_ref, ob_ref, og_ref, gn_ref, w_ref, x_ref, g_ref, o_ref, a_ref, *, nh):
    for h in range(nh):
        sl = slice(h * HG_HEAD, (h + 1) * HG_HEAD)
        o = of_ref[0, :, sl] + ob_ref[0, :, sl]
        o = o * lax.rsqrt(jnp.mean(o * o, axis=-1, keepdims=True) + EPS) * gn_ref[:, sl]
        a_ref[:, sl] = (o * _silu(og_ref[0, :, sl])).astype(BF16)
    o_ref[0] = x_ref[0] + g_ref[0] * _dot(a_ref[...], w_ref[...])


def _hgrn_out(o_f, o_b, u5, gn_g, w_out, x, gate):
    bn, l, d = x.shape
    nh = d // HG_HEAD
    tm = min(512, l)
    blk = lambda c: pl.BlockSpec((1, tm, d), lambda b, i: (b, i, c))
    return pl.pallas_call(
        functools.partial(_hgout_kernel, nh=nh),
        grid=(bn, l // tm),
        in_specs=[blk(0), blk(0), blk(2),
                  pl.BlockSpec((1, d), lambda b, i: (0, 0)),
                  pl.BlockSpec((d, d), lambda b, i: (0, 0)),
                  blk(0), pl.BlockSpec((1, 1, d), lambda b, i: (b, 0, 0))],
        out_specs=blk(0),
        out_shape=jax.ShapeDtypeStruct(x.shape, F32),
        scratch_shapes=[pltpu.VMEM((tm, d), BF16)],
        input_output_aliases={5: 0},
        compiler_params=_cp(("parallel", "parallel")),
        name="hgrn2_out",
    )(o_f, o_b, u5, gn_g.reshape(1, d), w_out, x, gate)


def _route_kernel(x_ref, g_ref, sc_ref, sh_ref, wr_ref, tok_ref, info_ref, cnt_ref, run_ref, *, tm):
    first = (pl.program_id(0) == 0) & (pl.program_id(1) == 0)

    @pl.when(first)
    def _():
        run_ref[...] = jnp.zeros_like(run_ref)

    h = _norm_mod(x_ref[0], g_ref[...], sc_ref[0], sh_ref[0])
    tok_ref[0] = h
    logits = _dot_acc(h, wr_ref[...])
    lane = lax.broadcasted_iota(jnp.int32, (tm, LANES), 1)
    lanef = lane.astype(F32)
    neg = jnp.float32(-jnp.inf)
    lg = jnp.where(lane < N_EXPERTS, logits, neg)
    m1 = jnp.max(lg, axis=-1, keepdims=True)
    i1 = jnp.min(jnp.where(lg == m1, lanef, float(LANES)), axis=-1, keepdims=True)
    lg2 = jnp.where(lanef == i1, neg, lg)
    m2 = jnp.max(lg2, axis=-1, keepdims=True)
    i2 = jnp.min(jnp.where(lg2 == m2, lanef, float(LANES)), axis=-1, keepdims=True)
    e2 = jnp.exp(m2 - m1)
    gate1 = 1.0 / (1.0 + e2)
    gate2 = e2 / (1.0 + e2)
    oh1 = (lanef == i1).astype(F32)
    oh2 = (lanef == i2).astype(F32)
    both = oh1 + oh2
    rr = lax.broadcasted_iota(jnp.int32, (tm, tm), 0)
    cc = lax.broadcasted_iota(jnp.int32, (tm, tm), 1)
    before = _dot((rr > cc).astype(BF16), both.astype(BF16)) + run_ref[...]
    r1 = jnp.sum(before * oh1, axis=-1, keepdims=True)
    r2 = jnp.sum(before * oh2, axis=-1, keepdims=True)
    run = run_ref[...] + jnp.sum(both, axis=0, keepdims=True)
    run_ref[...] = run
    cnt_ref[...] = jnp.broadcast_to(run, cnt_ref.shape)
    info = jnp.where(lane == 0, i1, jnp.where(lane == 1, i2, jnp.where(lane == 2, r1, jnp.where(
        lane == 3, r2, jnp.where(lane == 4, gate1, jnp.where(lane == 5, gate2, 0.0))))))
    info_ref[0] = info


def _moe_route(x, gain, sc1p, sh, w_router):
    bn, l, d = x.shape
    tm = min(256, l)
    wr = jnp.pad(w_router, ((0, 0), (0, LANES - w_router.shape[1])))
    vec = pl.BlockSpec((1, 1, d), lambda b, i: (b, 0, 0))
    return pl.pallas_call(
        functools.partial(_route_kernel, tm=tm),
        grid=(bn, l // tm),
        in_specs=[pl.BlockSpec((1, tm, d), lambda b, i: (b, i, 0)),
                  pl.BlockSpec((1, d), lambda b, i: (0, 0)), vec, vec,
                  pl.BlockSpec((d, LANES), lambda b, i: (0, 0))],
        out_specs=[pl.BlockSpec((1, tm, d), lambda b, i: (b, i, 0)),
                   pl.BlockSpec((1, tm, LANES), lambda b, i: (b, i, 0)),
                   pl.BlockSpec((8, LANES), lambda b, i: (0, 0))],
        out_shape=[jax.ShapeDtypeStruct((bn, l, d), F32),
                   jax.ShapeDtypeStruct((bn, l, LANES), F32),
                   jax.ShapeDtypeStruct((8, LANES), F32)],
        scratch_shapes=[pltpu.VMEM((1, LANES), F32)],
        compiler_params=_cp(("arbitrary", "arbitrary")),
        name="moe_route",
    )(x, gain.reshape(1, d), sc1p, sh, wr)


def _row_copy(src, dst, s, t, sem):
    return pltpu.make_async_copy(src.at[pl.ds(s, 1)], dst.at[pl.ds(t, 1)], sem)


def _dispatch_kernel(dest_ref, tok_ref, xb_in_ref, xb_ref, sem, *, tb):
    del xb_in_ref

    def start(g, _):
        for u in range(DMA_UNROLL):
            t = g * DMA_UNROLL + u
            _row_copy(tok_ref, xb_ref, t, dest_ref[0, 0, t], sem).start(priority=0)
            _row_copy(tok_ref, xb_ref, t, dest_ref[0, 1, t], sem).start(priority=1)
        return 0

    lax.fori_loop(0, tb // DMA_UNROLL, start, 0)
    for _ in range(2):
        pltpu.make_async_copy(tok_ref, xb_ref.at[pl.ds(0, tb)], sem).wait()


def _moe_dispatch(tok, dest, n_slots):
    t, d = tok.shape
    nb, _, tb = dest.shape
    xb0 = jnp.zeros((n_slots, d), F32)
    return pl.pallas_call(
        functools.partial(_dispatch_kernel, tb=tb),
        grid=(nb,),
        in_specs=[pl.BlockSpec((1, 2, tb), lambda i: (i, 0, 0), memory_space=pltpu.SMEM),
                  pl.BlockSpec((tb, d), lambda i: (i, 0)),
                  pl.BlockSpec(memory_space=pl.ANY)],
        out_specs=pl.BlockSpec(memory_space=pl.ANY),
        out_shape=jax.ShapeDtypeStruct((n_slots, d), F32),
        scratch_shapes=[pltpu.SemaphoreType.DMA(())],
        input_output_aliases={2: 0},
        compiler_params=_cp(("arbitrary",)),
        name="moe_dispatch",
    )(dest, tok, xb0)


def _exp_up_kernel(be_ref, x_ref, wg_ref, wu_ref, o_ref, xs_ref):
    del be_ref

    @pl.when(pl.program_id(1) == 0)
    def _():
        xs_ref[...] = x_ref[...].astype(BF16)

    x = xs_ref[...]
    o_ref[...] = (_silu(_dot(x, wg_ref[...])) * _dot(x, wu_ref[...])).astype(o_ref.dtype)


def _moe_up(xb, block_e, w_gate, w_up, blk):
    n_slots, d = xb.shape
    ff = w_gate.shape[2]
    tn = _col_tile(ff)
    wspec = pl.BlockSpec((None, d, tn), lambda i, j, be: (be[i], 0, j))
    grid_spec = pltpu.PrefetchScalarGridSpec(
        num_scalar_prefetch=1,
        grid=(n_slots // blk, ff // tn),
        in_specs=[pl.BlockSpec((blk, d), lambda i, j, be: (i, 0)), wspec, wspec],
        out_specs=pl.BlockSpec((blk, tn), lambda i, j, be: (i, j)),
        scratch_shapes=[pltpu.VMEM((blk, d), BF16)],
    )
    return pl.pallas_call(
        _exp_up_kernel,
        grid_spec=grid_spec,
        out_shape=jax.ShapeDtypeStruct((n_slots, ff), BF16),
        compiler_params=_cp(("parallel", "arbitrary")),
        name="moe_expert_up",
    )(block_e, xb, w_gate, w_up)


def _exp_down_kernel(be_ref, h_ref, w_ref, o_ref):
    del be_ref
    o_ref[...] = _dot(h_ref[...], w_ref[...])


def _moe_down(hmid, block_e, w_down, blk):
    n_slots, ff = hmid.shape
    d = w_down.shape[2]
    grid_spec = pltpu.PrefetchScalarGridSpec(
        num_scalar_prefetch=1,
        grid=(n_slots // blk,),
        in_specs=[pl.BlockSpec((blk, ff), lambda i, be: (i, 0)),
                  pl.BlockSpec((None, ff, d), lambda i, be: (be[i], 0, 0))],
        out_specs=pl.BlockSpec((blk, d), lambda i, be: (i, 0)),
    )
    return pl.pallas_call(
        _exp_down_kernel,
        grid_spec=grid_spec,
        out_shape=jax.ShapeDtypeStruct((n_slots, d), F32),
        compiler_params=_cp(("arbitrary",)),
        name="moe_expert_down",
    )(block_e, hmid, w_down)


def _combine_kernel(dest_ref, yb_ref, info_ref, x_ref, g_ref, fg_ref, o_ref, y_ref, sem, *, tb, final):
    def start(g, _):
        for u in range(DMA_UNROLL):
            t = g * DMA_UNROLL + u
            _row_copy(yb_ref, y_ref.at[0], dest_ref[0, 0, 0, t], t, sem).start(priority=0)
            _row_copy(yb_ref, y_ref.at[1], dest_ref[0, 0, 1, t], t, sem).start(priority=1)
        return 0

    lax.fori_loop(0, tb // DMA_UNROLL, start, 0)
    for k in range(2):
        pltpu.make_async_copy(yb_ref.at[pl.ds(0, tb)], y_ref.at[k], sem).wait()
    info = info_ref[0]
    lane = lax.broadcasted_iota(jnp.int32, info.shape, 1)
    gate1 = jnp.sum(jnp.where(lane == 4, info, 0.0), axis=-1, keepdims=True)
    gate2 = jnp.sum(jnp.where(lane == 5, info, 0.0), axis=-1, keepdims=True)
    xn = x_ref[0] + g_ref[0] * (gate1 * y_ref[0] + gate2 * y_ref[1])
    if final:
        xn = xn * lax.rsqrt(jnp.mean(xn * xn, axis=-1, keepdims=True) + EPS) * fg_ref[...]
    o_ref[0] = xn


def _moe_combine(yb, dest, info, x, gate, final_g):
    bn, l, d = x.shape
    tb = dest.shape[-1]
    nb = l // tb
    dest4 = dest.reshape(bn, nb, 2, tb)
    final = final_g is not None
    fg = (final_g if final else jnp.ones((d,), F32)).reshape(1, d)
    return pl.pallas_call(
        functools.partial(_combine_kernel, tb=tb, final=final),
        grid=(bn, nb),
        in_specs=[pl.BlockSpec((1, 1, 2, tb), lambda b, i: (b, i, 0, 0), memory_space=pltpu.SMEM),
                  pl.BlockSpec(memory_space=pl.ANY),
                  pl.BlockSpec((1, tb, LANES), lambda b, i: (b, i, 0)),
                  pl.BlockSpec((1, tb, d), lambda b, i: (b, i, 0)),
                  pl.BlockSpec((1, 1, d), lambda b, i: (b, 0, 0)),
                  pl.BlockSpec((1, d), lambda b, i: (0, 0))],
        out_specs=pl.BlockSpec((1, tb, d), lambda b, i: (b, i, 0)),
        out_shape=jax.ShapeDtypeStruct(x.shape, F32),
        scratch_shapes=[pltpu.VMEM((2, tb, d), F32), pltpu.SemaphoreType.DMA(())],
        input_output_aliases={3: 0},
        compiler_params=_cp(("arbitrary", "arbitrary")),
        name="moe_combine",
    )(dest4, yb, info, x, gate, fg)


def _moe_layer(x, gain, sc1p, sh, gate, w_router, w_gate, w_up, w_down, blk, final_g=None):
    bn, l, d = x.shape
    t = bn * l
    tok, info, cnt = _moe_route(x, gain, sc1p, sh, w_router)
    counts = cnt[0, :N_EXPERTS].astype(jnp.int32)
    padded = (counts + blk - 1) // blk * blk
    pad_end = jnp.cumsum(padded)
    pad_start = pad_end - padded
    n_blocks = -(-(2 * t) // blk) + N_EXPERTS
    block_e = jnp.minimum(
        jnp.sum((jnp.arange(n_blocks, dtype=jnp.int32)[:, None] * blk >= pad_end[None, :]).astype(jnp.int32), axis=1),
        N_EXPERTS - 1).astype(jnp.int32)
    info2 = info.reshape(t, LANES)
    ee = jnp.arange(N_EXPERTS, dtype=jnp.int32)[None, :]

    def slot(e_col, r_col):
        e = info2[:, e_col].astype(jnp.int32)
        return jnp.sum(jnp.where(e[:, None] == ee, pad_start[None, :], 0), axis=1) + info2[:, r_col].astype(jnp.int32)

    tb = min(256, l)
    dest = jnp.stack([slot(0, 2), slot(1, 3)], axis=0).reshape(2, t // tb, tb).transpose(1, 0, 2)
    xb = _moe_dispatch(tok.reshape(t, d), dest, n_blocks * blk)
    hmid = _moe_up(xb, block_e, w_gate, w_up, blk)
    yb = _moe_down(hmid, block_e, w_down, blk)
    return _moe_combine(yb, dest, info, x, gate, final_g)


def kernel(x, c, ctx, c_ctx, ada_w, ada_b, norm1_g, norm2_g, normf_g, cf_w_pw1, cf_b_pw1, cf_w_dw, cf_b_dw, cf_ln_g, cf_ln_b, cf_w_pw2, cf_b_pw2, hy_w_in, hy_b_in, hy_w_short, hy_b_short, hy_w_f1, hy_b_f1, hy_freq, hy_w_f2, hy_b_f2, hy_w_f3, hy_bias, hy_w_out, hy_b_out, hg_lb_logits, hg_w_in, hg_gn_g, hg_w_out, ffn_w_gate, ffn_w_up, ffn_w_down, moe_w_router, moe_w_gate, moe_w_up, moe_w_down):
    bn, s, d = x.shape
    cl = ctx.shape[1]
    depth = ada_w.shape[0]
    bf = lambda a: a.astype(BF16)

    cvecs = jnp.concatenate([c, c_ctx[None], jnp.zeros((8 - bn - 1, d), F32)], axis=0)
    mods = _ada_mod(cvecs, ada_w, ada_b)

    def mod_slices(i, ctx_rows):
        m = jnp.broadcast_to(mods[i, bn:bn + 1], (bn, N_MOD * d)) if ctx_rows else mods[i, :bn]
        parts = [m[:, k * d:(k + 1) * d].reshape(bn, 1, d) for k in range(N_MOD)]
        sh1, sc1, g1, sh2, sc2, g2 = parts
        return sh1, 1.0 + sc1, g1, sh2, 1.0 + sc2, g2

    p = jax.nn.softmax(hg_lb_logits.astype(F32), axis=0)
    lb_all = jnp.cumsum(p, axis=0) - p[0]

    x = _embed(x)
    xc = ctx

    for i in range(depth):
        last = i == depth - 1
        kind, j = i % 3, i // 3
        streams = [(x, mod_slices(i, False), False)]
        if not last:
            streams.append((xc, mod_slices(i, True), True))

        new = []
        if kind == 0:
            wa, wg = bf(cf_w_pw1[j][:, :d]), bf(cf_w_pw1[j][:, d:])
            ba, bg = cf_b_pw1[j][:d], cf_b_pw1[j][d:]
            w2 = bf(cf_w_pw2[j])
            for xs, (sh1, sc1, g1, _, _, _), _ in streams:
                u = _nmm(xs, norm1_g[i], sc1, sh1, [wa, wg], [ba, bg], epi="glu")
                new.append(_conformer_tail(u, cf_w_dw[j], cf_b_dw[j], cf_ln_g[j], cf_ln_b[j], w2, cf_b_pw2[j], xs, g1))
        elif kind == 1:
            order = hy_bias.shape[1]
            w_in_t = bf(hy_w_in[j].T)
            w_out = bf(hy_w_out[j])
            for xs, (sh1, sc1, g1, _, _, _), _ in streams:
                l = xs.shape[1]
                n2, r, nk, lp = _fft_plan(l)
                consts = _fft_consts(r, nk, n2)
                filt = _hyena_filters_t(l, lp, hy_w_f1[j], hy_b_f1[j], hy_freq[j], hy_w_f2[j], hy_b_f2[j],
                                        hy_w_f3[j], order, d)
                spec = _hyena_spectrum(filt, consts, r, nk, n2)
                ut = _nmm(xs, norm1_g[i], sc1, sh1, [w_in_t], [hy_b_in[j]], transposed=True)
                if lp > l:
                    ut = jnp.pad(ut, ((0, 0), (0, 0), (0, lp - l)))
                zt = _hyena_conv(ut, hy_w_short[j], hy_b_short[j], hy_bias[j], spec, consts, r, nk, n2, l)
                new.append(_mmres(zt, w_out, hy_b_out[j], xs, g1, trans_a=True))
        else:
            w_in = bf(hg_w_in[j])
            w_out = bf(hg_w_out[j])
            nh = d // HG_HEAD
            state = jnp.zeros((bn, 2, nh, HG_HEAD, HG_HEAD), F32)
            outs = {}
            for xs, (sh1, sc1, g1, _, _, _), is_ctx in ((xc, mod_slices(i, True), True), streams[0]):
                u5 = _nmm(xs, norm1_g[i], sc1, sh1, [w_in], [jnp.zeros((5 * d,), F32)])
                o_f, o_b, state = _hgrn_scan(u5, lb_all[i], state)
                if not (is_ctx and last):
                    outs[is_ctx] = _hgrn_out(o_f, o_b, u5, hg_gn_g[j], w_out, xs, g1)
            new = [outs[False]] + ([outs[True]] if not last else [])

        x = new[0]
        if not last:
            xc = new[1]
        streams = [(x, mod_slices(i, False), False)]
        if not last:
            streams.append((xc, mod_slices(i, True), True))

        new = []
        if i % 2 == 0:
            wg, wu, wd = bf(ffn_w_gate[i // 2]), bf(ffn_w_up[i // 2]), bf(ffn_w_down[i // 2])
            zb = jnp.zeros((wg.shape[1],), F32)
            for xs, (_, _, _, sh2, sc2, g2), _ in streams:
                hmid = _nmm(xs, norm2_g[i], sc2, sh2, [wg, wu], [zb, zb], epi="swiglu", out_dtype=BF16)
                new.append(_mmres(hmid, wd, jnp.zeros((d,), F32), xs, g2))
        else:
            wg, wu, wd = bf(moe_w_gate[i // 2]), bf(moe_w_up[i // 2]), bf(moe_w_down[i // 2])
            for xs, (_, _, _, sh2, sc2, g2), is_ctx in streams:
                new.append(_moe_layer(xs, norm2_g[i], sc2, sh2, g2, moe_w_router[i // 2], wg, wu, wd,
                                      blk=128 if is_ctx else 512, final_g=normf_g if last else None))
        x = new[0]
        if not last:
            xc = new[1]

    return x if (depth - 1) % 2 == 1 else _final_norm(x, normf_g)


def _fnorm_kernel(x_ref, g_ref, o_ref):
    xv = x_ref[0]
    o_ref[0] = xv * lax.rsqrt(jnp.mean(xv * xv, axis=-1, keepdims=True) + EPS) * g_ref[...]


def _final_norm(x, g):
    bn, l, d = x.shape
    tm = min(512, l)
    return pl.pallas_call(
        _fnorm_kernel,
        grid=(bn, l // tm),
        in_specs=[pl.BlockSpec((1, tm, d), lambda b, i: (b, i, 0)), pl.BlockSpec((1, d), lambda b, i: (0, 0))],
        out_specs=pl.BlockSpec((1, tm, d), lambda b, i: (b, i, 0)),
        out_shape=jax.ShapeDtypeStruct(x.shape, F32),
        compiler_params=_cp(("parallel", "parallel")),
        name="final_norm",
    )(x, g.reshape(1, d))
```
